```python
import jax, jax.numpy as jnp
from jax import lax
import numpy as np

D_MODEL = 1024
BATCH = 2
SEQ = 8192
DEPTH = 2
DEC_BATCH = 128
DEC_SEQ = 1
PAST_LEN = 16384
PAGE_SIZE = 128

N_BRANCH = 4
BRANCH_WIDTH = 512
MLA_HEADS = 8
MLA_NOPE = 64
MLA_ROPE = 32
MLA_V = 64
MLA_Q_RANK = 384
MLA_KV_RANK = 256
ROPE_BASE = 10000.0
POOL_WINDOWS = (2, 4, 8, 16)
POOL_GROUP = 128
POOL_WIDTH = POOL_GROUP * len(POOL_WINDOWS)
POOL_BUF = max(POOL_WINDOWS) - 1
MOBA_HEADS = 8
MOBA_KV_HEADS = 2
MOBA_HD = 64
MOBA_BLOCK = 256
MOBA_TOPK = 3
MEM_TOKENS = 256
MEM_HEADS = 4
MEM_HD = 128

Q_BLOCK = 128
NORM_EPS = 1e-6
NEG = -1e30
IN_SIZES = (MLA_Q_RANK, MLA_KV_RANK, MLA_ROPE, POOL_WIDTH, MOBA_HEADS * MOBA_HD, MOBA_KV_HEADS * MOBA_HD,
            MOBA_KV_HEADS * MOBA_HD, MEM_HEADS * MEM_HD, N_BRANCH * BRANCH_WIDTH, N_BRANCH * D_MODEL)
IN_SPLIT = tuple(sum(IN_SIZES[:i + 1]) for i in range(len(IN_SIZES) - 1))
D_IN = sum(IN_SIZES)

kernel_name = 'hybrid_mla_pool_moba_step'


def rmsnorm(x, g):
    xf = x.astype(jnp.float32)
    y = xf * lax.rsqrt(jnp.mean(xf * xf, axis=-1, keepdims=True) + NORM_EPS)
    return (y * g.astype(jnp.float32)).astype(x.dtype)


def rope(x, pos):
    half = x.shape[-1] // 2
    inv = ROPE_BASE ** (-jnp.arange(half, dtype=jnp.float32) / half)
    ang = pos.astype(jnp.float32)[:, None] * inv[None, :]
    shape = (1, pos.shape[0]) + (1,) * (x.ndim - 3) + (half,)
    cos = jnp.cos(ang).reshape(shape)
    sin = jnp.sin(ang).reshape(shape)
    xf = x.astype(jnp.float32)
    x1, x2 = xf[..., :half], xf[..., half:]
    return jnp.concatenate([x1 * cos - x2 * sin, x2 * cos + x1 * sin], axis=-1).astype(x.dtype)


def alibi_slopes(n):
    return 2.0 ** (-8.0 * jnp.arange(1, n + 1, dtype=jnp.float32) / n)


def mixer_inputs(x, ln_g, w_in):
    h = rmsnorm(x, ln_g)
    return jnp.split(jnp.einsum('btd,de->bte', h, w_in), IN_SPLIT, axis=-1)


def mla_queries(c_q, q_norm, w_uq, w_uk, pos):
    q = jnp.einsum('btr,rhe->bthe', rmsnorm(c_q, q_norm), w_uq)
    q_lat = jnp.einsum('bthn,chn->bthc', q[..., :MLA_NOPE], w_uk)
    return q_lat, rope(q[..., MLA_NOPE:], pos)


def mla_values(o_lat, w_uv):
    o = jnp.einsum('bthc,chv->bthv', o_lat, w_uv)
    return o.reshape(o.shape[0], o.shape[1], MLA_HEADS * MLA_V)


def mla_attend_prompt(q_lat, q_rope, ckv, kr):
    B, S, H, C = q_lat.shape
    nq = S // Q_BLOCK
    scale = (MLA_NOPE + MLA_ROPE) ** -0.5
    qlb = q_lat.reshape(B, nq, Q_BLOCK, H, C).swapaxes(0, 1)
    qrb = q_rope.reshape(B, nq, Q_BLOCK, H, MLA_ROPE).swapaxes(0, 1)
    key_pos = jnp.arange(S)

    def block(args):
        i, ql, qr = args
        q_pos = i * Q_BLOCK + jnp.arange(Q_BLOCK)
        s = jnp.einsum('bqhc,bkc->bhqk', ql, ckv) + jnp.einsum('bqhr,bkr->bhqk', qr, kr)
        s = jnp.where(key_pos[None, :] <= q_pos[:, None], s.astype(jnp.float32) * scale, NEG)
        p = jax.nn.softmax(s, axis=-1).astype(ckv.dtype)
        return jnp.einsum('bhqk,bkc->bqhc', p, ckv)

    o = lax.map(block, (jnp.arange(nq), qlb, qrb))
    return o.swapaxes(0, 1).reshape(B, S, H, C)


def mla_attend_sample(q_lat, q_rope, ckv_new, kr_new, cache_ckv, cache_kr, page_table, layer):
    DB, T, H, C = q_lat.shape
    past = page_table.shape[1] * PAGE_SIZE
    scale = (MLA_NOPE + MLA_ROPE) ** -0.5
    ckv_past = cache_ckv[layer, page_table].reshape(DB, past, C)
    kr_past = cache_kr[layer, page_table].reshape(DB, past, MLA_ROPE)
    s_past = jnp.einsum('bthc,bsc->bths', q_lat, ckv_past) + jnp.einsum('bthr,bsr->bths', q_rope, kr_past)
    s_new = jnp.einsum('bthc,bsc->bths', q_lat, ckv_new) + jnp.einsum('bthr,bsr->bths', q_rope, kr_new)
    causal = jnp.arange(T)[None, :] <= jnp.arange(T)[:, None]
    s_new = jnp.where(causal[None, :, None, :], s_new.astype(jnp.float32) * scale, NEG)
    s = jnp.concatenate([s_past.astype(jnp.float32) * scale, s_new], axis=-1)
    p = jax.nn.softmax(s, axis=-1).astype(ckv_new.dtype)
    return (jnp.einsum('bths,bsc->bthc', p[..., :past], ckv_past)
            + jnp.einsum('bths,bsc->bthc', p[..., past:], ckv_new))


def pool_mix(u_ext, n_prev, start_pos, w_pool, scale):
    B, L, W = u_ext.shape
    T = L - n_prev
    uf = u_ext.astype(jnp.float32)
    csum = jnp.concatenate([jnp.zeros((B, 1, W), jnp.float32), jnp.cumsum(uf, axis=1)], axis=1)
    hi = n_prev + 1 + jnp.arange(T)
    pos = start_pos + jnp.arange(T)
    parts = []
    for g, w in enumerate(POOL_WINDOWS):
        cs = csum[..., g * POOL_GROUP:(g + 1) * POOL_GROUP]
        lo = jnp.maximum(hi - w, 0)
        cnt = jnp.minimum(pos + 1, w).astype(jnp.float32)
        parts.append((cs[:, hi] - cs[:, lo]) / cnt[None, :, None])
    pooled = (jnp.concatenate(parts, axis=-1) - uf[:, n_prev:]).astype(u_ext.dtype)
    y = jnp.einsum('btgc,gce->btge', pooled.reshape(B, T, len(POOL_WINDOWS), POOL_GROUP), w_pool)
    return y.reshape(B, T, POOL_WIDTH) * scale


def moba_prompt(q, k, v, slopes):
    B, S, H, D = q.shape
    G = k.shape[2]
    rep = H // G
    scale = D ** -0.5
    nb = -(-S // MOBA_BLOCK)
    pad = nb * MOBA_BLOCK - S
    kb = jnp.pad(k, ((0, 0), (0, pad), (0, 0), (0, 0))).reshape(B, nb, MOBA_BLOCK, G, D)
    vb = jnp.pad(v, ((0, 0), (0, pad), (0, 0), (0, 0))).reshape(B, nb, MOBA_BLOCK, G, D)
    kmean = jnp.mean(kb.astype(jnp.float32), axis=2).astype(k.dtype)
    kb = kb.transpose(0, 1, 3, 2, 4)
    vb = vb.transpose(0, 1, 3, 2, 4)
    gate = jnp.einsum('bsgjd,bngd->bsgjn', q.reshape(B, S, G, rep, D), kmean).reshape(B, S, H, nb).astype(jnp.float32)
    own = jnp.arange(S) // MOBA_BLOCK
    gate = jnp.where((jnp.arange(nb)[None, :] < own[:, None])[None, :, None, :], gate, NEG)
    k_sel = min(MOBA_TOPK, nb)
    _, idx = lax.top_k(gate, k_sel)
    sel_ok = idx < own[None, :, None, None]
    nq = S // Q_BLOCK

    def to_blocks(a):
        return a.reshape((B, nq, Q_BLOCK) + a.shape[2:]).swapaxes(0, 1)

    b_ix = jnp.arange(B)[:, None, None, None]
    g_ix = (jnp.arange(H) // rep)[None, None, :, None]
    rows = jnp.arange(MOBA_BLOCK)
    n_sel = k_sel * MOBA_BLOCK

    def block(args):
        i, qb, ib, okb = args
        q_pos = i * Q_BLOCK + jnp.arange(Q_BLOCK)
        kg = kb[b_ix, ib, g_ix]
        vg = vb[b_ix, ib, g_ix]
        key_pos = ib[..., None] * MOBA_BLOCK + rows
        dist = (q_pos[None, :, None, None, None] - key_pos).astype(jnp.float32)
        s_sel = jnp.einsum('bqhd,bqhkrd->bqhkr', qb, kg).astype(jnp.float32) * scale - slopes[None, None, :, None, None] * dist
        s_sel = jnp.where(okb[..., None], s_sel, NEG).reshape(B, Q_BLOCK, H, n_sel)
        ob = (i * Q_BLOCK) // MOBA_BLOCK
        k_own = lax.dynamic_index_in_dim(kb, ob, axis=1, keepdims=False)
        v_own = lax.dynamic_index_in_dim(vb, ob, axis=1, keepdims=False)
        d_own = (q_pos[:, None] - (ob * MOBA_BLOCK + rows)[None, :]).astype(jnp.float32)
        qg = qb.reshape(B, Q_BLOCK, G, rep, D)
        s_own = (jnp.einsum('bqgjd,bgkd->bqgjk', qg, k_own).reshape(B, Q_BLOCK, H, MOBA_BLOCK).astype(jnp.float32) * scale
                 - slopes[None, None, :, None] * d_own[None, :, None, :])
        s_own = jnp.where((d_own >= 0)[None, :, None, :], s_own, NEG)
        p = jax.nn.softmax(jnp.concatenate([s_sel, s_own], axis=-1), axis=-1).astype(v.dtype)
        o_sel = jnp.einsum('bqhkr,bqhkrd->bqhd', p[..., :n_sel].reshape(B, Q_BLOCK, H, k_sel, MOBA_BLOCK), vg)
        o_own = jnp.einsum('bqgjk,bgkd->bqgjd', p[..., n_sel:].reshape(B, Q_BLOCK, G, rep, MOBA_BLOCK), v_own)
        return o_sel + o_own.reshape(B, Q_BLOCK, H, D)

    o = lax.map(block, (jnp.arange(nq), to_blocks(q), to_blocks(idx), to_blocks(sel_ok)))
    return o.swapaxes(0, 1).reshape(B, S, H * D)


def moba_sample(q, k_new, v_new, cache_k, cache_v, page_table, layer, slopes):
    DB, T, H, D = q.shape
    G = k_new.shape[2]
    rep = H // G
    scale = D ** -0.5
    n_pages = page_table.shape[1]
    past = n_pages * PAGE_SIZE
    ppb = MOBA_BLOCK // PAGE_SIZE
    n_full = past // MOBA_BLOCK
    q_pos = past + jnp.arange(T)
    own = q_pos // MOBA_BLOCK
    rows = jnp.arange(MOBA_BLOCK)
    sl = slopes[None, None, :, None]
    qg = q.reshape(DB, T, G, rep, D)
    own_lpage = jnp.minimum(own[:, None] * ppb + jnp.arange(ppb), n_pages - 1)
    own_phys = page_table[:, own_lpage]
    k_op = cache_k[layer, own_phys].reshape(DB, T, MOBA_BLOCK, G, D)
    v_op = cache_v[layer, own_phys].reshape(DB, T, MOBA_BLOCK, G, D)
    op_pos = own[:, None] * MOBA_BLOCK + rows
    d_op = (q_pos[:, None] - op_pos).astype(jnp.float32)
    s_op = (jnp.einsum('btgjd,btkgd->btgjk', qg, k_op).reshape(DB, T, H, MOBA_BLOCK).astype(jnp.float32) * scale
            - sl * d_op[None, :, None, :])
    s_op = jnp.where((op_pos < past)[None, :, None, :], s_op, NEG)
    d_new = (q_pos[:, None] - q_pos[None, :]).astype(jnp.float32)
    s_new = (jnp.einsum('btgjd,bsgd->btgjs', qg, k_new).reshape(DB, T, H, T).astype(jnp.float32) * scale
             - sl * d_new[None, :, None, :])
    s_new = jnp.where((d_new >= 0)[None, :, None, :], s_new, NEG)
    k_sel = min(MOBA_TOPK, n_full)
    n_sel = k_sel * MOBA_BLOCK
    if k_sel > 0:
        k_full = cache_k[layer, page_table[:, :n_full * ppb]].reshape(DB, n_full, MOBA_BLOCK, G, D)
        kmean = jnp.mean(k_full.astype(jnp.float32), axis=2).astype(k_new.dtype)
        gate = jnp.einsum('btgjd,bngd->btgjn', qg, kmean).reshape(DB, T, H, n_full).astype(jnp.float32)
        gate = jnp.where((jnp.arange(n_full)[None, :] < own[:, None])[None, :, None, :], gate, NEG)
        _, idx = lax.top_k(gate, k_sel)
        ok = idx < own[None, :, None, None]
        lpage = idx[..., None] * ppb + jnp.arange(ppb)
        phys = page_table[jnp.arange(DB)[:, None, None, None, None], lpage]
        g_ix = (jnp.arange(H) // rep)[None, None, :, None, None]
        kg = cache_k[layer, phys, :, g_ix].reshape(DB, T, H, k_sel, MOBA_BLOCK, D)
        vg = cache_v[layer, phys, :, g_ix].reshape(DB, T, H, k_sel, MOBA_BLOCK, D)
        key_pos = idx[..., None] * MOBA_BLOCK + rows
        dist = (q_pos[None, :, None, None, None] - key_pos).astype(jnp.float32)
        s_sel = jnp.einsum('bthd,bthkrd->bthkr', q, kg).astype(jnp.float32) * scale - slopes[None, None, :, None, None] * dist
        s_sel = jnp.where(ok[..., None], s_sel, NEG).reshape(DB, T, H, n_sel)
        s = jnp.concatenate([s_sel, s_op, s_new], axis=-1)
    else:
        s = jnp.concatenate([s_op, s_new], axis=-1)
    p = jax.nn.softmax(s, axis=-1).astype(v_new.dtype)
    p_op = p[..., n_sel:n_sel + MOBA_BLOCK].reshape(DB, T, G, rep, MOBA_BLOCK)
    p_new = p[..., n_sel + MOBA_BLOCK:].reshape(DB, T, G, rep, T)
    o = jnp.einsum('btgjk,btkgd->btgjd', p_op, v_op) + jnp.einsum('btgjs,bsgd->btgjd', p_new, v_new)
    o = o.reshape(DB, T, H, D)
    if k_sel > 0:
        o = o + jnp.einsum('bthkr,bthkrd->bthd', p[..., :n_sel].reshape(DB, T, H, k_sel, MOBA_BLOCK), vg)
    return o.reshape(DB, T, H * D)


def mem_kv(mem, g, w):
    kv = jnp.einsum('bmd,de->bme', rmsnorm(mem, g), w)
    k, v = jnp.split(kv, 2, axis=-1)
    B, M = mem.shape[:2]
    return k.reshape(B, M, MEM_HEADS, MEM_HD), v.reshape(B, M, MEM_HEADS, MEM_HD)


def mem_attend(q, k, v):
    B, T = q.shape[:2]
    qh = q.reshape(B, T, MEM_HEADS, MEM_HD)
    s = jnp.einsum('bthd,bmhd->bhtm', qh, k).astype(jnp.float32) * MEM_HD ** -0.5
    p = jax.nn.softmax(s, axis=-1).astype(v.dtype)
    return jnp.einsum('bhtm,bmhd->bthd', p, v).reshape(B, T, MEM_HEADS * MEM_HD)


def merge(branches, z, gate_logits, w_branch, w_out):
    B, T = z.shape[:2]
    o = jnp.stack(branches, axis=2) * jax.nn.silu(z.reshape(B, T, N_BRANCH, BRANCH_WIDTH))
    proj = jnp.einsum('btnw,nwd->btnd', o, w_branch)
    g = jax.nn.sigmoid(gate_logits.reshape(B, T, N_BRANCH, D_MODEL))
    return jnp.einsum('btd,de->bte', jnp.sum(g * proj, axis=2), w_out)


def setup_inputs(seed: int = 0) -> dict:
    key = jax.random.key(seed)
    ks = jax.random.split(key, 32)
    n_pages = PAST_LEN // PAGE_SIZE
    n_used = DEC_BATCH * n_pages
    n_phys = n_used + n_used // 4

    def nrm(k, shape, scale=1.0):
        return jax.random.normal(k, shape, jnp.float32) * scale

    def gain(k, shape):
        return 1.0 + 0.05 * jax.random.normal(k, shape, jnp.float32)

    return {
        'x_prompt': nrm(ks[0], (BATCH, SEQ, D_MODEL)),
        'x_sample': nrm(ks[1], (DEC_BATCH, DEC_SEQ, D_MODEL)),
        'cache_mla_ckv': nrm(ks[2], (DEPTH, n_phys, PAGE_SIZE, MLA_KV_RANK)),
        'cache_mla_krope': nrm(ks[3], (DEPTH, n_phys, PAGE_SIZE, MLA_ROPE)),
        'cache_moba_k': nrm(ks[4], (DEPTH, n_phys, PAGE_SIZE, MOBA_KV_HEADS, MOBA_HD)),
        'cache_moba_v': nrm(ks[5], (DEPTH, n_phys, PAGE_SIZE, MOBA_KV_HEADS, MOBA_HD)),
        'state_pool': nrm(ks[6], (DEPTH, DEC_BATCH, POOL_BUF, POOL_WIDTH)),
        'cache_mem_k': nrm(ks[7], (DEPTH, DEC_BATCH, MEM_TOKENS, MEM_HEADS, MEM_HD)),
        'cache_mem_v': nrm(ks[8], (DEPTH, DEC_BATCH, MEM_TOKENS, MEM_HEADS, MEM_HD)),
        'page_table': jax.random.permutation(ks[9], n_phys)[:n_used].reshape(DEC_BATCH, n_pages).astype(jnp.int32),
        'mem_prompt': nrm(ks[10], (BATCH, MEM_TOKENS, D_MODEL)),
        'ln_g': gain(ks[11], (DEPTH, D_MODEL)),
        'w_in': nrm(ks[12], (DEPTH, D_MODEL, D_IN), D_MODEL ** -0.5),
        'mla_q_norm': gain(ks[13], (DEPTH, MLA_Q_RANK)),
        'mla_w_uq': nrm(ks[14], (DEPTH, MLA_Q_RANK, MLA_HEADS, MLA_NOPE + MLA_ROPE), MLA_Q_RANK ** -0.5),
        'mla_kv_norm': gain(ks[15], (DEPTH, MLA_KV_RANK)),
        'mla_w_uk': nrm(ks[16], (DEPTH, MLA_KV_RANK, MLA_HEADS, MLA_NOPE), MLA_KV_RANK ** -0.5),
        'mla_w_uv': nrm(ks[17], (DEPTH, MLA_KV_RANK, MLA_HEADS, MLA_V), MLA_KV_RANK ** -0.5),
        'pool_w': nrm(ks[18], (DEPTH, len(POOL_WINDOWS), POOL_GROUP, POOL_GROUP), POOL_GROUP ** -0.5),
        'pool_scale': 1.0 + 0.1 * nrm(ks[19], (DEPTH, POOL_WIDTH)),
        'mem_norm': gain(ks[20], (DEPTH, D_MODEL)),
        'w_mem_kv': nrm(ks[21], (DEPTH, D_MODEL, 2 * MEM_HEADS * MEM_HD), D_MODEL ** -0.5),
        'w_branch': nrm(ks[22], (DEPTH, N_BRANCH, BRANCH_WIDTH, D_MODEL), BRANCH_WIDTH ** -0.5),
        'w_out': nrm(ks[23], (DEPTH, D_MODEL, D_MODEL), D_MODEL ** -0.5),
        'final_norm': gain(ks[24], (D_MODEL,)),
    }


def reference(x_prompt, x_sample, cache_mla_ckv, cache_mla_krope, cache_moba_k, cache_moba_v, state_pool,
              cache_mem_k, cache_mem_v, page_table, mem_prompt, ln_g, w_in, mla_q_norm, mla_w_uq, mla_kv_norm,
              mla_w_uk, mla_w_uv, pool_w, pool_scale, mem_norm, w_mem_kv, w_branch, w_out, final_norm):
    B, S, _ = x_prompt.shape
    DB, T, _ = x_sample.shape
    past = page_table.shape[1] * PAGE_SIZE
    pos_p = jnp.arange(S)
    pos_s = past + jnp.arange(T)
    slopes = alibi_slopes(MOBA_HEADS)
    xp, xs = x_prompt, x_sample
    ckv_p, kr_p, mbk_p, mbv_p, pool_p, memk_p, memv_p = [], [], [], [], [], [], []
    ckv_s, kr_s, mbk_s, mbv_s, pool_s = [], [], [], [], []
    for l in range(DEPTH):
        c_q, c_kv, k_rope, u, q_mb, k_mb, v_mb, q_mem, z, gl = mixer_inputs(xp, ln_g[l], w_in[l])
        q_lat, q_rope = mla_queries(c_q, mla_q_norm[l], mla_w_uq[l], mla_w_uk[l], pos_p)
        ckv = rmsnorm(c_kv, mla_kv_norm[l])
        kr = rope(k_rope, pos_p)
        o_mla = mla_values(mla_attend_prompt(q_lat, q_rope, ckv, kr), mla_w_uv[l])
        o_pool = pool_mix(u, 0, 0, pool_w[l], pool_scale[l])
        k_mb = k_mb.reshape(B, S, MOBA_KV_HEADS, MOBA_HD)
        v_mb = v_mb.reshape(B, S, MOBA_KV_HEADS, MOBA_HD)
        o_moba = moba_prompt(q_mb.reshape(B, S, MOBA_HEADS, MOBA_HD), k_mb, v_mb, slopes)
        mem_k, mem_v = mem_kv(mem_prompt, mem_norm[l], w_mem_kv[l])
        o_mem = mem_attend(q_mem, mem_k, mem_v)
        xp = xp + merge([o_mla, o_pool, o_moba, o_mem], z, gl, w_branch[l], w_out[l])
        ckv_p.append(ckv)
        kr_p.append(kr)
        mbk_p.append(k_mb)
        mbv_p.append(v_mb)
        pool_p.append(u[:, S - POOL_BUF:])
        memk_p.append(mem_k)
        memv_p.append(mem_v)
        c_q, c_kv, k_rope, u, q_mb, k_mb, v_mb, q_mem, z, gl = mixer_inputs(xs, ln_g[l], w_in[l])
        q_lat, q_rope = mla_queries(c_q, mla_q_norm[l], mla_w_uq[l], mla_w_uk[l], pos_s)
        ckv = rmsnorm(c_kv, mla_kv_norm[l])
        kr = rope(k_rope, pos_s)
        o_mla = mla_values(mla_attend_sample(q_lat, q_rope, ckv, kr, cache_mla_ckv, cache_mla_krope, page_table, l), mla_w_uv[l])
        u_ext = jnp.concatenate([state_pool[l].astype(u.dtype), u], axis=1)
        o_pool = pool_mix(u_ext, POOL_BUF, past, pool_w[l], pool_scale[l])
        k_mb = k_mb.reshape(DB, T, MOBA_KV_HEADS, MOBA_HD)
        v_mb = v_mb.reshape(DB, T, MOBA_KV_HEADS, MOBA_HD)
        o_moba = moba_sample(q_mb.reshape(DB, T, MOBA_HEADS, MOBA_HD), k_mb, v_mb, cache_moba_k, cache_moba_v,
                             page_table, l, slopes)
        o_mem = mem_attend(q_mem, cache_mem_k[l], cache_mem_v[l])
        xs = xs + merge([o_mla, o_pool, o_moba, o_mem], z, gl, w_branch[l], w_out[l])
        ckv_s.append(ckv)
        kr_s.append(kr)
        mbk_s.append(k_mb)
        mbv_s.append(v_mb)
        pool_s.append(u_ext[:, u_ext.shape[1] - POOL_BUF:])
    y_prompt = rmsnorm(xp, final_norm)
    y_sample = rmsnorm(xs, final_norm)
    new_mla_ckv_prompt = jnp.stack(ckv_p)
    new_mla_krope_prompt = jnp.stack(kr_p)
    new_moba_k_prompt = jnp.stack(mbk_p)
    new_moba_v_prompt = jnp.stack(mbv_p)
    new_pool_prompt = jnp.stack(pool_p)
    new_mem_k_prompt = jnp.stack(memk_p)
    new_mem_v_prompt = jnp.stack(memv_p)
    new_mla_ckv_sample = jnp.stack(ckv_s)
    new_mla_krope_sample = jnp.stack(kr_s)
    new_moba_k_sample = jnp.stack(mbk_s)
    new_moba_v_sample = jnp.stack(mbv_s)
    new_pool_sample = jnp.stack(pool_s)
    return (y_prompt, y_sample, new_mla_ckv_prompt, new_mla_krope_prompt, new_moba_k_prompt, new_moba_v_prompt,
            new_pool_prompt, new_mem_k_prompt, new_mem_v_prompt, new_mla_ckv_sample, new_mla_krope_sample,
            new_moba_k_sample, new_moba_v_sample, new_pool_sample)
```

```python
import functools

import jax
import jax.numpy as jnp
from jax import lax
from jax.experimental import pallas as pl
from jax.experimental.pallas import tpu as pltpu

F32 = jnp.float32
BF16 = jnp.bfloat16
I32 = jnp.int32

D_MODEL = 1024
PAGE_SIZE = 128
N_BRANCH = 4
BRANCH_WIDTH = 512
MLA_HEADS = 8
MLA_NOPE = 64
MLA_ROPE = 32
MLA_V = 64
MLA_Q_RANK = 384
MLA_KV_RANK = 256
ROPE_BASE = 10000.0
POOL_WINDOWS = (2, 4, 8, 16)
POOL_GROUP = 128
POOL_WIDTH = POOL_GROUP * len(POOL_WINDOWS)
POOL_BUF = max(POOL_WINDOWS) - 1
MOBA_HEADS = 8
MOBA_KV_HEADS = 2
MOBA_HD = 64
MOBA_BLOCK = 256
MOBA_TOPK = 3
MEM_HEADS = 4
MEM_HD = 128
NORM_EPS = 1e-6
NEG = -1e30
KNOCKOUT = -3.0e38
IN_SIZES = (MLA_Q_RANK, MLA_KV_RANK, MLA_ROPE, POOL_WIDTH, MOBA_HEADS * MOBA_HD, MOBA_KV_HEADS * MOBA_HD,
            MOBA_KV_HEADS * MOBA_HD, MEM_HEADS * MEM_HD, N_BRANCH * BRANCH_WIDTH, N_BRANCH * D_MODEL)

LANES = 128
KCAT = MLA_KV_RANK + LANES
HALF = LANES // 2
VMEM_LIMIT = 56 * 1024 * 1024

A_CQ = 0
A_CKV = A_CQ + MLA_Q_RANK
A_U = A_CKV + MLA_KV_RANK
A_QMB = A_U + POOL_WIDTH
A_KMB = A_QMB + MOBA_HEADS * MOBA_HD
A_VMB = A_KMB + LANES
A_QMEM = A_VMB + LANES
A_KRA = A_QMEM + MEM_HEADS * MEM_HD
A_KRB = A_KRA + LANES
A_END = A_KRB + LANES


def _rms(x, g):
    return x * lax.rsqrt(jnp.mean(x * x, axis=-1, keepdims=True) + NORM_EPS) * g


def _bdot(a, b):
    return jnp.dot(a.astype(BF16), b.astype(BF16), preferred_element_type=F32)


def _bdot_nt(a, b):
    return lax.dot_general(a.astype(BF16), b.astype(BF16), (((1,), (1,)), ((), ())), preferred_element_type=F32)


def _sigmoid(x):
    return 1.0 / (1.0 + jnp.exp(-x))


def _const_spec(shape):
    nd = len(shape)
    return pl.BlockSpec(shape, lambda *_: (0,) * nd, pipeline_mode=pl.Buffered(1))


def _params(n_axes=1):
    return pltpu.CompilerParams(dimension_semantics=("arbitrary",) * n_axes, vmem_limit_bytes=VMEM_LIMIT)


def _top_k_mask(gate, colf, k):
    sel = jnp.zeros_like(gate)
    picked = []
    big = jnp.float32(gate.shape[-1])
    for _ in range(k):
        mx = jnp.max(gate, axis=-1, keepdims=True)
        idx = jnp.min(jnp.where(gate == mx, colf, big), axis=-1, keepdims=True)
        hit = colf == idx
        sel = jnp.where(hit, 1.0, sel)
        gate = jnp.where(hit, KNOCKOUT, gate)
        picked.append(idx)
    return sel, picked


def _dup_halves(a, low):
    r = pltpu.roll(a, HALF, 1)
    return jnp.concatenate([jnp.where(low, a, r), jnp.where(low, r, a)], axis=-1)


def _proj_kernel(*refs, blocks_per_tile):
    (x_ref, g_ref, wa_ref, qn_ref, kvn_ref, wuq_ref, mh_ref, cos_ref, sin_ref,
     ckv_ref, kr_ref, kcat_ref, qcat_ref, u_ref, qmb_ref, kmb_ref, vmb_ref, k2_ref, v2_ref, qmem_ref) = refs[:20]
    tm = x_ref.shape[0]
    h = _rms(x_ref[...], g_ref[...])
    y = _bdot(h, wa_ref[...])
    cos = cos_ref[...]
    sin = sin_ref[...]
    ckv = _rms(y[:, A_CKV:A_U], kvn_ref[...])
    ckv_ref[...] = ckv
    kr = y[:, A_KRA:A_KRB] * cos[:, :LANES] + y[:, A_KRB:A_END] * sin[:, :LANES]
    kr_ref[...] = kr[:, :MLA_ROPE]
    kcat_ref[...] = jnp.concatenate([ckv.astype(BF16), kr.astype(BF16)], axis=-1)
    q1 = _bdot(_rms(y[:, A_CQ:A_CKV], qn_ref[...]), wuq_ref[...])
    n_nope = MLA_HEADS * MLA_NOPE
    n_rope = MLA_HEADS * MLA_ROPE
    roped = q1[:, n_nope:n_nope + n_rope] * cos + q1[:, n_nope + n_rope:] * sin
    for hd in range(MLA_HEADS):
        pair = (hd * MLA_NOPE) // LANES
        quad = (hd * MLA_ROPE) // LANES
        lhs = jnp.concatenate([q1[:, pair * LANES:(pair + 1) * LANES], roped[:, quad * LANES:(quad + 1) * LANES]], axis=-1)
        qcat_ref[hd] = _bdot(lhs, mh_ref[hd]).astype(BF16)
    u_ref[...] = y[:, A_U:A_QMB]
    qmb_ref[...] = (y[:, A_QMB:A_KMB] * (MOBA_HD ** -0.5)).astype(BF16)
    kmb = y[:, A_KMB:A_VMB]
    vmb = y[:, A_VMB:A_QMEM]
    kmb_ref[...] = kmb
    vmb_ref[...] = vmb
    low = lax.broadcasted_iota(I32, (tm, LANES), 1) < HALF
    k2_ref[...] = _dup_halves(kmb, low).astype(BF16)
    v2_ref[...] = _dup_halves(vmb, low).astype(BF16)
    qmem_ref[...] = y[:, A_QMEM:A_KRA].astype(BF16)
    if blocks_per_tile:
        km2_ref = refs[20]
        low1 = lax.broadcasted_iota(I32, (1, LANES), 1) < HALF
        for j in range(blocks_per_tile):
            mean = jnp.sum(kmb[j * MOBA_BLOCK:(j + 1) * MOBA_BLOCK], axis=0, keepdims=True) * (1.0 / MOBA_BLOCK)
            km2_ref[j] = _dup_halves(mean, low1)


def _project(x, lw, cos, sin, tm, with_kmean):
    n = x.shape[0]
    nt = n // tm
    npos = cos.shape[0] // tm
    bpt = tm // MOBA_BLOCK if with_kmean else 0
    row = lambda w: pl.BlockSpec((tm, w), lambda i: (i, 0))
    in_specs = [row(D_MODEL), _const_spec((1, D_MODEL)), _const_spec((D_MODEL, A_END)), _const_spec((1, MLA_Q_RANK)),
                _const_spec((1, MLA_KV_RANK)), _const_spec(lw["wuq"].shape), _const_spec(lw["mh"].shape),
                pl.BlockSpec((tm, 2 * LANES), lambda i: (i % npos, 0)), pl.BlockSpec((tm, 2 * LANES), lambda i: (i % npos, 0))]
    out_shape = [jax.ShapeDtypeStruct((n, MLA_KV_RANK), F32), jax.ShapeDtypeStruct((n, MLA_ROPE), F32),
                 jax.ShapeDtypeStruct((n, KCAT), BF16), jax.ShapeDtypeStruct((MLA_HEADS, n, KCAT), BF16),
                 jax.ShapeDtypeStruct((n, POOL_WIDTH), F32), jax.ShapeDtypeStruct((n, MOBA_HEADS * MOBA_HD), BF16),
                 jax.ShapeDtypeStruct((n, LANES), F32), jax.ShapeDtypeStruct((n, LANES), F32),
                 jax.ShapeDtypeStruct((n, 2 * LANES), BF16), jax.ShapeDtypeStruct((n, 2 * LANES), BF16),
                 jax.ShapeDtypeStruct((n, MEM_HEADS * MEM_HD), BF16)]
    out_specs = [row(MLA_KV_RANK), row(MLA_ROPE), row(KCAT), pl.BlockSpec((MLA_HEADS, tm, KCAT), lambda i: (0, i, 0)),
                 row(POOL_WIDTH), row(MOBA_HEADS * MOBA_HD), row(LANES), row(LANES), row(2 * LANES), row(2 * LANES),
                 row(MEM_HEADS * MEM_HD)]
    if bpt:
        out_shape.append(jax.ShapeDtypeStruct((n // MOBA_BLOCK, 1, 2 * LANES), F32))
        out_specs.append(pl.BlockSpec((bpt, 1, 2 * LANES), lambda i: (i, 0, 0)))
    outs = pl.pallas_call(
        functools.partial(_proj_kernel, blocks_per_tile=bpt),
        grid=(nt,), in_specs=in_specs, out_specs=out_specs, out_shape=out_shape,
        compiler_params=_params(), name="proj",
    )(x, lw["ln_g"], lw["wa"], lw["q_norm"], lw["kv_norm"], lw["wuq"], lw["mh"], cos, sin)
    names = ["ckv", "kr", "kcat", "qcat", "u", "qmb", "kmb", "vmb", "k2", "v2", "qmem"] + (["km2"] if bpt else [])
    return dict(zip(names, outs))


def _mla_prompt_kernel(q_ref, k_ref, wuv_ref, o_ref, m_ref, l_ref, acc_ref, *, tq, tk, scale):
    i = pl.program_id(1)
    rows = MLA_HEADS * tq
    q = q_ref[...].reshape(rows, KCAT)
    jd = (i * tq) // tk

    def tile(j):
        return k_ref[pl.ds(pl.multiple_of(j * tk, tk), tk), :]

    kd = tile(jd)
    s = _bdot_nt(q, kd) * scale
    qpos = i * tq + (lax.broadcasted_iota(I32, (rows, tk), 0) & (tq - 1))
    kpos = jd * tk + lax.broadcasted_iota(I32, (rows, tk), 1)
    s = jnp.where(kpos <= qpos, s, NEG)
    m = jnp.max(s, axis=-1, keepdims=True)
    p = jnp.exp(s - m)
    m_ref[...] = m
    l_ref[...] = jnp.sum(p, axis=-1, keepdims=True)
    acc_ref[...] = jnp.dot(p.astype(BF16), kd[:, :MLA_KV_RANK], preferred_element_type=F32)

    def body(j, carry):
        kj = tile(j)
        sj = _bdot_nt(q, kj) * scale
        m_old = m_ref[...]
        m_new = jnp.maximum(m_old, jnp.max(sj, axis=-1, keepdims=True))
        alpha = jnp.exp(m_old - m_new)
        pj = jnp.exp(sj - m_new)
        l_ref[...] = alpha * l_ref[...] + jnp.sum(pj, axis=-1, keepdims=True)
        acc_ref[...] = alpha * acc_ref[...] + jnp.dot(pj.astype(BF16), kj[:, :MLA_KV_RANK], preferred_element_type=F32)
        m_ref[...] = m_new
        return carry

    lax.fori_loop(0, jd, body, 0)
    o_lat = acc_ref[...] / l_ref[...]
    out = jnp.zeros((tq, MLA_HEADS * MLA_V), F32)
    for hd in range(MLA_HEADS):
        out = out + _bdot(o_lat[hd * tq:(hd + 1) * tq], wuv_ref[hd])
    o_ref[...] = out


def _mla_prompt(qcat, kcat, wuv, batch, seq, tq, tk):
    nq = seq // tq
    rows = MLA_HEADS * tq
    return pl.pallas_call(
        functools.partial(_mla_prompt_kernel, tq=tq, tk=tk, scale=(MLA_NOPE + MLA_ROPE) ** -0.5),
        grid=(batch, nq),
        in_specs=[pl.BlockSpec((MLA_HEADS, tq, KCAT), lambda b, i: (0, b * nq + i, 0)),
                  pl.BlockSpec((seq, KCAT), lambda b, i: (b, 0)),
                  _const_spec(wuv.shape)],
        out_specs=pl.BlockSpec((tq, MLA_HEADS * MLA_V), lambda b, i: (b * nq + i, 0)),
        out_shape=jax.ShapeDtypeStruct((batch * seq, MLA_HEADS * MLA_V), F32),
        scratch_shapes=[pltpu.VMEM((rows, 1), F32), pltpu.VMEM((rows, 1), F32), pltpu.VMEM((rows, MLA_KV_RANK), F32)],
        compiler_params=_params(2), name="mla_prompt",
    )(qcat, kcat, wuv)


def _moba_prompt_kernel(q_ref, k2_ref, v2_ref, km2_ref, slope_ref, o_ref,
                        qs_ref, sel_ref, m_ref, l_ref, acc_ref, *, tq, nb, topk):
    i = pl.program_id(1)
    q0 = i * tq
    ob = q0 // MOBA_BLOCK
    rg = (MOBA_HEADS // MOBA_KV_HEADS) * tq
    lane = lax.broadcasted_iota(I32, (tq, LANES), 1)
    for hd in range(MOBA_HEADS):
        pair = q_ref[:, (hd // 2) * LANES:(hd // 2 + 1) * LANES]
        keep = (lane < HALF) if hd % 2 == 0 else (lane >= HALF)
        qs_ref[hd * tq:(hd + 1) * tq, :] = jnp.where(keep, pair, jnp.zeros_like(pair))
    km = km2_ref[...].astype(BF16)
    colf = lax.broadcasted_iota(I32, (rg, nb), 1).astype(F32)
    past_col = colf < ob.astype(F32)
    for g in range(MOBA_KV_HEADS):
        gate = _bdot_nt(qs_ref[g * rg:(g + 1) * rg, :], km[:, g * LANES:(g + 1) * LANES])
        sel, _ = _top_k_mask(jnp.where(past_col, gate, NEG), colf, topk)
        sel_ref[g * rg:(g + 1) * rg, :] = jnp.where(past_col, sel, 0.0)

    qrow = q0 + (lax.broadcasted_iota(I32, (rg, MOBA_BLOCK), 0) & (tq - 1))
    kcol = lax.broadcasted_iota(I32, (rg, MOBA_BLOCK), 1)

    def scores(g, jb):
        start = pl.multiple_of(jb * MOBA_BLOCK, MOBA_BLOCK)
        kb = k2_ref[pl.ds(start, MOBA_BLOCK), g * LANES:(g + 1) * LANES]
        vb = v2_ref[pl.ds(start, MOBA_BLOCK), g * LANES:(g + 1) * LANES]
        dist = (qrow - (jb * MOBA_BLOCK + kcol)).astype(F32)
        s = _bdot_nt(qs_ref[g * rg:(g + 1) * rg, :], kb) - slope_ref[g * rg:(g + 1) * rg, :] * dist
        return s, dist, vb

    for g in range(MOBA_KV_HEADS):
        rs = slice(g * rg, (g + 1) * rg)
        s, dist, vb = scores(g, ob)
        s = jnp.where(dist >= 0.0, s, NEG)
        m = jnp.max(s, axis=-1, keepdims=True)
        p = jnp.exp(s - m)
        m_ref[rs, :] = m
        l_ref[rs, :] = jnp.sum(p, axis=-1, keepdims=True)
        acc_ref[rs, :] = jnp.dot(p.astype(BF16), vb, preferred_element_type=F32)

    def body(jb, carry):
        jf = jb.astype(F32)
        for g in range(MOBA_KV_HEADS):
            rs = slice(g * rg, (g + 1) * rg)
            s, _, vb = scores(g, jb)
            chosen = jnp.sum(jnp.where(colf == jf, sel_ref[rs, :], 0.0), axis=-1, keepdims=True) > 0.5
            s = jnp.where(chosen, s, NEG)
            m_old = m_ref[rs, :]
            m_new = jnp.maximum(m_old, jnp.max(s, axis=-1, keepdims=True))
            alpha = jnp.exp(m_old - m_new)
            p = jnp.exp(s - m_new)
            l_ref[rs, :] = alpha * l_ref[rs, :] + jnp.sum(p, axis=-1, keepdims=True)
            acc_ref[rs, :] = alpha * acc_ref[rs, :] + jnp.dot(p.astype(BF16), vb, preferred_element_type=F32)
            m_ref[rs, :] = m_new
        return carry

    lax.fori_loop(0, ob, body, 0)
    for pr in range(MOBA_HEADS // 2):
        ra = slice(2 * pr * tq, (2 * pr + 1) * tq)
        rb = slice((2 * pr + 1) * tq, (2 * pr + 2) * tq)
        oa = acc_ref[ra, :] / l_ref[ra, :]
        ob_ = acc_ref[rb, :] / l_ref[rb, :]
        o_ref[:, pr * LANES:(pr + 1) * LANES] = jnp.where(lane < HALF, oa, ob_)


def _moba_prompt(qmb, k2, v2, km2, slopes, batch, seq, tq):
    nq = seq // tq
    nb = seq // MOBA_BLOCK
    rows = MOBA_HEADS * tq
    slope_col = jnp.repeat(slopes, tq)[:, None]
    return pl.pallas_call(
        functools.partial(_moba_prompt_kernel, tq=tq, nb=nb, topk=min(MOBA_TOPK, nb)),
        grid=(batch, nq),
        in_specs=[pl.BlockSpec((tq, MOBA_HEADS * MOBA_HD), lambda b, i: (b * nq + i, 0)),
                  pl.BlockSpec((seq, 2 * LANES), lambda b, i: (b, 0)),
                  pl.BlockSpec((seq, 2 * LANES), lambda b, i: (b, 0)),
                  pl.BlockSpec((nb, 2 * LANES), lambda b, i: (b, 0)),
                  _const_spec((rows, 1))],
        out_specs=pl.BlockSpec((tq, MOBA_HEADS * MOBA_HD), lambda b, i: (b * nq + i, 0)),
        out_shape=jax.ShapeDtypeStruct((batch * seq, MOBA_HEADS * MOBA_HD), F32),
        scratch_shapes=[pltpu.VMEM((rows, LANES), BF16), pltpu.VMEM((rows, nb), F32), pltpu.VMEM((rows, 1), F32),
                        pltpu.VMEM((rows, 1), F32), pltpu.VMEM((rows, LANES), F32)],
        compiler_params=_params(2), name="moba_prompt",
    )(qmb, k2, v2, km2, slope_col)


HALO = 16


def _pool_prompt_kernel(u_ref, halo_ref, w_ref, sc_ref, o_ref, ext_ref, *, tiles_per_seq):
    tm = u_ref.shape[0]
    i = pl.program_id(0) % tiles_per_seq
    u = u_ref[...]
    ext_ref[0:HALO, :] = jnp.where(i > 0, halo_ref[...], 0.0)
    ext_ref[HALO:HALO + tm, :] = u
    pos = i * tm + lax.broadcasted_iota(I32, (tm, 1), 0)
    parts = []
    for g, w in enumerate(POOL_WINDOWS):
        cols = slice(g * POOL_GROUP, (g + 1) * POOL_GROUP)
        tot = u[:, cols]
        for k in range(1, w):
            tot = tot + ext_ref[HALO - k:HALO - k + tm, cols]
        cnt = jnp.minimum(pos + 1, w).astype(F32)
        parts.append(_bdot(tot / cnt - u[:, cols], w_ref[g]))
    o_ref[...] = jnp.concatenate(parts, axis=-1) * sc_ref[...]


def _pool_prompt(u, w_pool, scale, seq, tm):
    n = u.shape[0]
    tps = seq // tm
    hpt = tm // HALO
    return pl.pallas_call(
        functools.partial(_pool_prompt_kernel, tiles_per_seq=tps),
        grid=(n // tm,),
        in_specs=[pl.BlockSpec((tm, POOL_WIDTH), lambda i: (i, 0)),
                  pl.BlockSpec((HALO, POOL_WIDTH), lambda i: (jnp.maximum(i * hpt - 1, 0), 0)),
                  _const_spec(w_pool.shape), _const_spec((1, POOL_WIDTH))],
        out_specs=pl.BlockSpec((tm, POOL_WIDTH), lambda i: (i, 0)),
        out_shape=jax.ShapeDtypeStruct((n, POOL_WIDTH), F32),
        scratch_shapes=[pltpu.VMEM((HALO + tm, POOL_WIDTH), F32)],
        compiler_params=_params(), name="pool_prompt",
    )(u, u, w_pool, scale)


def _pool_sample_kernel(st_ref, u_ref, w_ref, sc_ref, o_ref, *, start_pos):
    u = u_ref[...]
    parts = []
    for g, w in enumerate(POOL_WINDOWS):
        cols = slice(g * POOL_GROUP, (g + 1) * POOL_GROUP)
        tot = u[:, cols]
        for k in range(1, w):
            tot = tot + st_ref[POOL_BUF - k][:, cols]
        parts.append(_bdot(tot / float(min(start_pos + 1, w)) - u[:, cols], w_ref[g]))
    o_ref[...] = jnp.concatenate(parts, axis=-1) * sc_ref[...]


def _pool_sample(state_t, u, w_pool, scale, start_pos):
    n = u.shape[0]
    return pl.pallas_call(
        functools.partial(_pool_sample_kernel, start_pos=start_pos),
        grid=(1,),
        in_specs=[_const_spec(state_t.shape), _const_spec(u.shape), _const_spec(w_pool.shape), _const_spec((1, POOL_WIDTH))],
        out_specs=_const_spec((n, POOL_WIDTH)),
        out_shape=jax.ShapeDtypeStruct((n, POOL_WIDTH), F32),
        compiler_params=_params(), name="pool_sample",
    )(state_t, u, w_pool, scale)


def _mem_kv_kernel(mem_ref, g_ref, w_ref, k_ref, v_ref):
    kv = _bdot(_rms(mem_ref[...], g_ref[...]), w_ref[...])
    half = MEM_HEADS * MEM_HD
    k_ref[...] = kv[:, :half]
    v_ref[...] = kv[:, half:]


def _mem_kv(mem, g, w):
    n = mem.shape[0]
    half = MEM_HEADS * MEM_HD
    return pl.pallas_call(
        _mem_kv_kernel, grid=(1,),
        in_specs=[_const_spec(mem.shape), _const_spec((1, D_MODEL)), _const_spec(w.shape)],
        out_specs=[_const_spec((n, half)), _const_spec((n, half))],
        out_shape=[jax.ShapeDtypeStruct((n, half), F32), jax.ShapeDtypeStruct((n, half), F32)],
        compiler_params=_params(), name="mem_kv",
    )(mem, g, w)


def _mem_prompt_kernel(q_ref, k_ref, v_ref, o_ref):
    q = q_ref[...]
    outs = []
    for hd in range(MEM_HEADS):
        cols = slice(hd * MEM_HD, (hd + 1) * MEM_HD)
        s = _bdot_nt(q[:, cols], k_ref[:, cols]) * (MEM_HD ** -0.5)
        e = jnp.exp(s - jnp.max(s, axis=-1, keepdims=True))
        p = e / jnp.sum(e, axis=-1, keepdims=True)
        outs.append(_bdot(p, v_ref[:, cols]))
    o_ref[...] = jnp.concatenate(outs, axis=-1)


def _mem_prompt(qmem, mem_k, mem_v, seq, tm):
    n = qmem.shape[0]
    tps = seq // tm
    m_tok = mem_k.shape[0] // (n // seq)
    width = MEM_HEADS * MEM_HD
    return pl.pallas_call(
        _mem_prompt_kernel, grid=(n // tm,),
        in_specs=[pl.BlockSpec((tm, width), lambda i: (i, 0)),
                  pl.BlockSpec((m_tok, width), lambda i: (i // tps, 0)),
                  pl.BlockSpec((m_tok, width), lambda i: (i // tps, 0))],
        out_specs=pl.BlockSpec((tm, width), lambda i: (i, 0)),
        out_shape=jax.ShapeDtypeStruct((n, width), F32),
        compiler_params=_params(), name="mem_prompt",
    )(qmem, mem_k, mem_v)


def _mem_sample_kernel(q_ref, k_ref, v_ref, o_ref):
    sb = q_ref.shape[0]
    q = q_ref[...].astype(F32)
    for b in range(sb):
        prod = k_ref[b] * q[b:b + 1, :]
        for hd in range(MEM_HEADS):
            cols = slice(hd * MEM_HD, (hd + 1) * MEM_HD)
            s = jnp.sum(prod[:, cols], axis=-1, keepdims=True) * (MEM_HD ** -0.5)
            e = jnp.exp(s - jnp.max(s, axis=0, keepdims=True))
            o = jnp.sum(e * v_ref[b, :, cols], axis=0, keepdims=True) / jnp.sum(e, axis=0, keepdims=True)
            o_ref[b:b + 1, cols] = o


def _mem_sample(qmem, mem_k, mem_v, sb):
    n, m_tok, width = mem_k.shape
    return pl.pallas_call(
        _mem_sample_kernel, grid=(n // sb,),
        in_specs=[pl.BlockSpec((sb, width), lambda i: (i, 0)),
                  pl.BlockSpec((sb, m_tok, width), lambda i: (i, 0, 0)),
                  pl.BlockSpec((sb, m_tok, width), lambda i: (i, 0, 0))],
        out_specs=pl.BlockSpec((sb, width), lambda i: (i, 0)),
        out_shape=jax.ShapeDtypeStruct((n, width), F32),
        compiler_params=_params(), name="mem_sample",
    )(qmem, mem_k, mem_v)


def _merge_kernel(x_ref, g_ref, wz_ref, wgl_ref, wbr_ref, wout_ref, fn_ref, b0_ref, b1_ref, b2_ref, b3_ref, o_ref, *, final):
    x = x_ref[...]
    h = _rms(x, g_ref[...]).astype(BF16)
    acc = jnp.zeros(x.shape, F32)
    for b, br in enumerate((b0_ref, b1_ref, b2_ref, b3_ref)):
        z = jnp.dot(h, wz_ref[:, b * BRANCH_WIDTH:(b + 1) * BRANCH_WIDTH], preferred_element_type=F32)
        gl = jnp.dot(h, wgl_ref[:, b * D_MODEL:(b + 1) * D_MODEL], preferred_element_type=F32)
        o = br[...] * (z * _sigmoid(z))
        acc = acc + _sigmoid(gl) * _bdot(o, wbr_ref[b])
    y = x + _bdot(acc, wout_ref[...])
    if final:
        y = _rms(y, fn_ref[...])
    o_ref[...] = y


def _merge(x, lw, final_norm, branches, tm, final):
    n = x.shape[0]
    row = lambda w: pl.BlockSpec((tm, w), lambda i: (i, 0))
    return pl.pallas_call(
        functools.partial(_merge_kernel, final=final),
        grid=(n // tm,),
        in_specs=[row(D_MODEL), _const_spec((1, D_MODEL)), _const_spec(lw["wz"].shape), _const_spec(lw["wgl"].shape),
                  _const_spec(lw["wbr"].shape), _const_spec(lw["wout"].shape), _const_spec((1, D_MODEL))]
                 + [row(BRANCH_WIDTH)] * N_BRANCH,
        out_specs=row(D_MODEL),
        out_shape=jax.ShapeDtypeStruct((n, D_MODEL), F32),
        compiler_params=_params(), name="merge",
    )(x, lw["ln_g"], lw["wz"], lw["wgl"], lw["wbr"], lw["wout"], final_norm, *branches)


def _chunking(n_pages):
    ch = 16 if n_pages % 32 == 0 else n_pages // 2
    assert ch >= 1 and n_pages % (2 * ch) == 0, n_pages
    return ch, n_pages // ch


def _mla_sample_kernel(pt_ref, q_ref, kn_ref, ckv_hbm, kr_hbm, o_ref, kbuf, rbuf, sem, *, layer, ch, nch, scale):
    b = pl.program_id(0)
    nb = pl.num_programs(0)

    def copies(bb, c, slot):
        out = []
        for p in range(ch):
            page = pt_ref[bb, c * ch + p]
            dst = pl.ds(p * PAGE_SIZE, PAGE_SIZE)
            out.append(pltpu.make_async_copy(ckv_hbm.at[layer, page], kbuf.at[slot, dst], sem.at[0, slot]))
            out.append(pltpu.make_async_copy(kr_hbm.at[layer, page], rbuf.at[slot, dst], sem.at[1, slot]))
        return out

    @pl.when(b == 0)
    def _():
        for cp in copies(0, 0, 0):
            cp.start()

    q = q_ref[0]
    kn = kn_ref[0]
    m = jnp.sum(q.astype(F32) * kn.astype(F32), axis=-1, keepdims=True) * scale
    l = jnp.ones_like(m)
    acc = jnp.broadcast_to(kn[:, :MLA_KV_RANK].astype(F32), (MLA_HEADS, MLA_KV_RANK))
    for c in range(nch):
        slot = c % 2
        if c + 1 < nch:
            for cp in copies(b, c + 1, 1 - slot):
                cp.start()
        else:
            @pl.when(b + 1 < nb)
            def _():
                for cp in copies(b + 1, 0, 1 - slot):
                    cp.start()
        for cp in copies(b, c, slot):
            cp.wait()
        kb = kbuf[slot].astype(BF16)
        rb = rbuf[slot].astype(BF16)
        s = (_bdot_nt(q[:, :MLA_KV_RANK], kb) + _bdot_nt(q[:, MLA_KV_RANK:MLA_KV_RANK + MLA_ROPE], rb)) * scale
        m_new = jnp.maximum(m, jnp.max(s, axis=-1, keepdims=True))
        alpha = jnp.exp(m - m_new)
        p = jnp.exp(s - m_new)
        l = alpha * l + jnp.sum(p, axis=-1, keepdims=True)
        acc = alpha * acc + jnp.dot(p.astype(BF16), kb, preferred_element_type=F32)
        m = m_new
    o_ref[0] = acc / l


def _mla_sample(page_table, q, knew, cache_ckv, cache_kr, layer):
    ndb, n_pages = page_table.shape
    ch, nch = _chunking(n_pages)
    rows = ch * PAGE_SIZE
    return pl.pallas_call(
        functools.partial(_mla_sample_kernel, layer=layer, ch=ch, nch=nch, scale=(MLA_NOPE + MLA_ROPE) ** -0.5),
        grid_spec=pltpu.PrefetchScalarGridSpec(
            num_scalar_prefetch=1, grid=(ndb,),
            in_specs=[pl.BlockSpec((1, MLA_HEADS, KCAT), lambda b, pt: (b, 0, 0)),
                      pl.BlockSpec((1, 1, KCAT), lambda b, pt: (b, 0, 0)),
                      pl.BlockSpec(memory_space=pl.ANY), pl.BlockSpec(memory_space=pl.ANY)],
            out_specs=pl.BlockSpec((1, MLA_HEADS, MLA_KV_RANK), lambda b, pt: (b, 0, 0)),
            scratch_shapes=[pltpu.VMEM((2, rows, MLA_KV_RANK), F32), pltpu.VMEM((2, rows, MLA_ROPE), F32),
                            pltpu.SemaphoreType.DMA((2, 2))]),
        out_shape=jax.ShapeDtypeStruct((ndb, MLA_HEADS, MLA_KV_RANK), F32),
        compiler_params=_params(), name="mla_sample",
    )(page_table, q, knew, cache_ckv, cache_kr)


def _uv_kernel(o_ref, wuv_ref, y_ref):
    out = jnp.zeros(y_ref.shape, F32)
    for hd in range(MLA_HEADS):
        out = out + _bdot(o_ref[hd], wuv_ref[hd])
    y_ref[...] = out


def _uv_project(o_lat, wuv):
    n = o_lat.shape[1]
    return pl.pallas_call(
        _uv_kernel, grid=(1,),
        in_specs=[_const_spec(o_lat.shape), _const_spec(wuv.shape)],
        out_specs=_const_spec((n, MLA_HEADS * MLA_V)),
        out_shape=jax.ShapeDtypeStruct((n, MLA_HEADS * MLA_V), F32),
        compiler_params=_params(), name="mla_uv",
    )(o_lat, wuv)


def _moba_gate_kernel(pt_ref, q_ref, k_hbm, idx_ref, kbuf, km_ref, sem, *, layer, ch, nch, own, topk):
    b = pl.program_id(0)
    nb = pl.num_programs(0)
    rep = MOBA_HEADS // MOBA_KV_HEADS

    def copies(bb, c, slot):
        return [pltpu.make_async_copy(k_hbm.at[layer, pt_ref[bb, c * ch + p]],
                                      kbuf.at[slot, pl.ds(p * PAGE_SIZE, PAGE_SIZE)], sem.at[slot]) for p in range(ch)]

    @pl.when(b == 0)
    def _():
        for cp in copies(0, 0, 0):
            cp.start()

    bpc = ch * PAGE_SIZE // MOBA_BLOCK
    for c in range(nch):
        slot = c % 2
        if c + 1 < nch:
            for cp in copies(b, c + 1, 1 - slot):
                cp.start()
        else:
            @pl.when(b + 1 < nb)
            def _():
                for cp in copies(b + 1, 0, 1 - slot):
                    cp.start()
        for cp in copies(b, c, slot):
            cp.wait()
        for g in range(MOBA_KV_HEADS):
            x = kbuf[slot, :, g, :]
            km_ref[g, c * bpc:(c + 1) * bpc, :] = jnp.sum(x.reshape(bpc, MOBA_BLOCK, MOBA_HD), axis=1) * (1.0 / MOBA_BLOCK)
    q = q_ref[0]
    nblk = km_ref.shape[1]
    g0 = _bdot_nt(q, km_ref[0])
    g1 = _bdot_nt(q, km_ref[1])
    head = lax.broadcasted_iota(I32, (MOBA_HEADS, nblk), 0)
    colf = lax.broadcasted_iota(I32, (MOBA_HEADS, nblk), 1).astype(F32)
    gate = jnp.where(head < rep, g0, g1)
    gate = jnp.where(colf < float(own), gate, NEG)
    _, picked = _top_k_mask(gate, colf, topk)
    lane = lax.broadcasted_iota(I32, (MOBA_HEADS, LANES), 1)
    out = jnp.zeros((MOBA_HEADS, LANES), F32)
    for t, idx in enumerate(picked):
        out = jnp.where(lane == t, idx, out)
    idx_ref[0] = out.astype(I32)


def _moba_gate(page_table, q, cache_k, layer, n_full, own, topk):
    ndb, n_pages = page_table.shape
    ppb = MOBA_BLOCK // PAGE_SIZE
    n_stream = n_full * ppb
    ch, nch = _chunking(n_stream)
    return pl.pallas_call(
        functools.partial(_moba_gate_kernel, layer=layer, ch=ch, nch=nch, own=own, topk=topk),
        grid_spec=pltpu.PrefetchScalarGridSpec(
            num_scalar_prefetch=1, grid=(ndb,),
            in_specs=[pl.BlockSpec((1, MOBA_HEADS, MOBA_HD), lambda b, pt: (b, 0, 0)),
                      pl.BlockSpec(memory_space=pl.ANY)],
            out_specs=pl.BlockSpec((1, MOBA_HEADS, LANES), lambda b, pt: (b, 0, 0)),
            scratch_shapes=[pltpu.VMEM((2, ch * PAGE_SIZE, MOBA_KV_HEADS, MOBA_HD), F32),
                            pltpu.VMEM((MOBA_KV_HEADS, n_full, MOBA_HD), F32),
                            pltpu.SemaphoreType.DMA((2,))]),
        out_shape=jax.ShapeDtypeStruct((ndb, MOBA_HEADS, LANES), I32),
        compiler_params=_params(), name="moba_gate",
    )(page_table, q, cache_k)


def _moba_sample_kernel(pt_ref, idx_ref, q_ref, kn_ref, vn_ref, slope_ref, k_hbm, v_hbm, o_ref, kbuf, vbuf, sem,
                        *, layer, topk, own, q_pos):
    b = pl.program_id(0)
    nb = pl.num_programs(0)
    ppb = MOBA_BLOCK // PAGE_SIZE
    rep = MOBA_HEADS // MOBA_KV_HEADS

    def copies(bb, slot):
        out = []
        for hd in range(MOBA_HEADS):
            for t in range(topk):
                blk = idx_ref[bb, hd * topk + t]
                for j in range(ppb):
                    page = pt_ref[bb, blk * ppb + j]
                    dst = pl.ds(((hd * topk + t) * ppb + j) * PAGE_SIZE, PAGE_SIZE)
                    out.append(pltpu.make_async_copy(k_hbm.at[layer, page], kbuf.at[slot, dst], sem.at[0, slot]))
                    out.append(pltpu.make_async_copy(v_hbm.at[layer, page], vbuf.at[slot, dst], sem.at[1, slot]))
        return out

    slot = b % 2

    @pl.when(b == 0)
    def _():
        for cp in copies(0, 0):
            cp.start()

    @pl.when(b + 1 < nb)
    def _():
        for cp in copies(b + 1, 1 - slot):
            cp.start()

    for cp in copies(b, slot):
        cp.wait()

    q = q_ref[0]
    nsel = topk * MOBA_BLOCK
    lane = lax.broadcasted_iota(I32, (1, nsel), 1)
    within = lane & (MOBA_BLOCK - 1)
    for hd in range(MOBA_HEADS):
        g = hd // rep
        qh = q[hd:hd + 1, :]
        kh = kbuf[slot, pl.ds(hd * nsel, nsel), g, :]
        vh = vbuf[slot, pl.ds(hd * nsel, nsel), g, :]
        blk_of_lane = jnp.zeros((1, nsel), I32)
        for t in range(topk):
            blk_of_lane = jnp.where(lane >= t * MOBA_BLOCK, idx_ref[b, hd * topk + t], blk_of_lane)
        dist = (q_pos - (blk_of_lane * MOBA_BLOCK + within)).astype(F32)
        slope = slope_ref[hd:hd + 1, :]
        s = _bdot_nt(qh, kh) - slope * dist
        s = jnp.where(blk_of_lane < own, s, NEG)
        s_self = jnp.sum(qh.astype(F32) * kn_ref[0, g:g + 1, :].astype(BF16).astype(F32), axis=-1, keepdims=True)
        m = jnp.maximum(jnp.max(s, axis=-1, keepdims=True), s_self)
        p = jnp.exp(s - m)
        p_self = jnp.exp(s_self - m)
        den = jnp.sum(p, axis=-1, keepdims=True) + p_self
        o = (_bdot(p, vh) + p_self * vn_ref[0, g:g + 1, :]) / den
        o_ref[0, hd:hd + 1, :] = o


def _moba_sample(page_table, idx, q, knew, vnew, slopes, cache_k, cache_v, layer, topk, own, q_pos):
    ndb = page_table.shape[0]
    ppb = MOBA_BLOCK // PAGE_SIZE
    rows = MOBA_HEADS * topk * ppb * PAGE_SIZE
    grp = (1, MOBA_KV_HEADS, MOBA_HD)
    return pl.pallas_call(
        functools.partial(_moba_sample_kernel, layer=layer, topk=topk, own=own, q_pos=q_pos),
        grid_spec=pltpu.PrefetchScalarGridSpec(
            num_scalar_prefetch=2, grid=(ndb,),
            in_specs=[pl.BlockSpec((1, MOBA_HEADS, MOBA_HD), lambda b, pt, ix: (b, 0, 0)),
                      pl.BlockSpec(grp, lambda b, pt, ix: (b, 0, 0)),
                      pl.BlockSpec(grp, lambda b, pt, ix: (b, 0, 0)),
                      pl.BlockSpec((MOBA_HEADS, 1), lambda b, pt, ix: (0, 0)),
                      pl.BlockSpec(memory_space=pl.ANY), pl.BlockSpec(memory_space=pl.ANY)],
            out_specs=pl.BlockSpec((1, MOBA_HEADS, MOBA_HD), lambda b, pt, ix: (b, 0, 0)),
            scratch_shapes=[pltpu.VMEM((2, rows, MOBA_KV_HEADS, MOBA_HD), F32),
                            pltpu.VMEM((2, rows, MOBA_KV_HEADS, MOBA_HD), F32),
                            pltpu.SemaphoreType.DMA((2, 2))]),
        out_shape=jax.ShapeDtypeStruct((ndb, MOBA_HEADS, MOBA_HD), F32),
        compiler_params=_params(), name="moba_sample",
    )(page_table, idx, q, knew, vnew, slopes[:, None], cache_k, cache_v)


def _rope_tables(pos):
    half = MLA_ROPE // 2
    inv = ROPE_BASE ** (-jnp.arange(half, dtype=F32) / half)
    ang = pos.astype(F32)[:, None] * inv[None, :]
    cos, sin = jnp.cos(ang), jnp.sin(ang)
    reps = 2 * LANES // MLA_ROPE
    return jnp.tile(jnp.concatenate([cos, cos], axis=-1), (1, reps)), jnp.tile(jnp.concatenate([-sin, sin], axis=-1), (1, reps))


def _pad_lanes(w):
    return jnp.pad(w, ((0, 0), (0, LANES - w.shape[1])))


def _layer_weights(l, ln_g, w_in, mla_q_norm, mla_w_uq, mla_kv_norm, mla_w_uk, mla_w_uv, pool_w, pool_scale,
                   mem_norm, w_mem_kv, w_branch, w_out):
    half = MLA_ROPE // 2
    swap = jnp.concatenate([jnp.arange(half, MLA_ROPE), jnp.arange(half)])
    offs = [0]
    for s in IN_SIZES:
        offs.append(offs[-1] + s)
    c_q, c_kv, k_rope, u, q_mb, k_mb, v_mb, q_mem, z, gl = [w_in[l][:, offs[i]:offs[i + 1]] for i in range(len(IN_SIZES))]
    wa = jnp.concatenate([c_q, c_kv, u, q_mb, k_mb, v_mb, q_mem, _pad_lanes(k_rope), _pad_lanes(k_rope[:, swap])], axis=1)
    uq = mla_w_uq[l]
    qr = uq.shape[0]
    wuq = jnp.concatenate([uq[:, :, :MLA_NOPE].reshape(qr, -1), uq[:, :, MLA_NOPE:].reshape(qr, -1),
                           uq[:, :, MLA_NOPE:][:, :, swap].reshape(qr, -1)], axis=1)
    mh = jnp.zeros((MLA_HEADS, 2 * LANES, KCAT), F32)
    eye = jnp.eye(MLA_ROPE, dtype=F32)
    for hd in range(MLA_HEADS):
        r0 = (hd * MLA_NOPE) % LANES
        mh = mh.at[hd, r0:r0 + MLA_NOPE, :MLA_KV_RANK].set(mla_w_uk[l][:, hd, :].T)
        r1 = LANES + (hd * MLA_ROPE) % LANES
        mh = mh.at[hd, r1:r1 + MLA_ROPE, MLA_KV_RANK:MLA_KV_RANK + MLA_ROPE].set(eye)
    wuv = jnp.zeros((MLA_HEADS, MLA_KV_RANK, MLA_HEADS * MLA_V), F32)
    for hd in range(MLA_HEADS):
        wuv = wuv.at[hd, :, hd * MLA_V:(hd + 1) * MLA_V].set(mla_w_uv[l][:, hd, :])
    return dict(
        ln_g=ln_g[l][None, :], wa=wa.astype(BF16), q_norm=mla_q_norm[l][None, :], kv_norm=mla_kv_norm[l][None, :],
        wuq=wuq.astype(BF16), mh=mh.astype(BF16), wuv=wuv.astype(BF16), wz=z.astype(BF16), wgl=gl.astype(BF16),
        wbr=w_branch[l].astype(BF16), wout=w_out[l].astype(BF16), pool_w=pool_w[l].astype(BF16),
        pool_scale=pool_scale[l][None, :], mem_norm=mem_norm[l][None, :], w_mem_kv=w_mem_kv[l].astype(BF16))


def kernel(x_prompt, x_sample, cache_mla_ckv, cache_mla_krope, cache_moba_k, cache_moba_v, state_pool, cache_mem_k, cache_mem_v, page_table, mem_prompt, ln_g, w_in, mla_q_norm, mla_w_uq, mla_kv_norm, mla_w_uk, mla_w_uv, pool_w, pool_scale, mem_norm, w_mem_kv, w_branch, w_out, final_norm):
    batch, seq, _ = x_prompt.shape
    ndb, t_new, _ = x_sample.shape
    depth = ln_g.shape[0]
    n_pages = page_table.shape[1]
    past = n_pages * PAGE_SIZE
    m_tok = mem_prompt.shape[1]
    assert t_new == 1, "one new token per sample"
    assert past % MOBA_BLOCK == 0, "no partially filled cached MoBA block"
    assert seq % MOBA_BLOCK == 0 and ndb % 8 == 0
    tm = min(256, seq)
    tq = 128
    tk = min(512, seq)
    n_full = past // MOBA_BLOCK
    own_s = past // MOBA_BLOCK
    topk_s = min(MOBA_TOPK, n_full)
    slopes = 2.0 ** (-8.0 * jnp.arange(1, MOBA_HEADS + 1, dtype=F32) / MOBA_HEADS)
    cos_p, sin_p = _rope_tables(jnp.arange(seq))
    cos_s, sin_s = _rope_tables(jnp.full((ndb,), past))
    fnorm = final_norm[None, :]

    xp = x_prompt.reshape(batch * seq, D_MODEL)
    xs = x_sample.reshape(ndb, D_MODEL)
    mem_flat = mem_prompt.reshape(batch * m_tok, D_MODEL)
    outs = {k: [] for k in ("ckv_p", "kr_p", "mbk_p", "mbv_p", "pool_p", "memk_p", "memv_p",
                            "ckv_s", "kr_s", "mbk_s", "mbv_s", "pool_s")}
    for l in range(depth):
        lw = _layer_weights(l, ln_g, w_in, mla_q_norm, mla_w_uq, mla_kv_norm, mla_w_uk, mla_w_uv, pool_w, pool_scale,
                            mem_norm, w_mem_kv, w_branch, w_out)
        last = l == depth - 1
        pj = _project(xp, lw, cos_p, sin_p, tm, True)
        o_mla = _mla_prompt(pj["qcat"], pj["kcat"], lw["wuv"], batch, seq, tq, tk)
        o_pool = _pool_prompt(pj["u"], lw["pool_w"], lw["pool_scale"], seq, tm)
        km2 = pj["km2"].reshape(batch * seq // MOBA_BLOCK, 2 * LANES)
        o_moba = _moba_prompt(pj["qmb"], pj["k2"], pj["v2"], km2, slopes, batch, seq, tq)
        mem_k, mem_v = _mem_kv(mem_flat, lw["mem_norm"], lw["w_mem_kv"])
        o_mem = _mem_prompt(pj["qmem"], mem_k, mem_v, seq, tm)
        xp = _merge(xp, lw, fnorm, (o_mla, o_pool, o_moba, o_mem), tm, last)
        outs["ckv_p"].append(pj["ckv"].reshape(batch, seq, MLA_KV_RANK))
        outs["kr_p"].append(pj["kr"].reshape(batch, seq, MLA_ROPE))
        outs["mbk_p"].append(pj["kmb"].reshape(batch, seq, MOBA_KV_HEADS, MOBA_HD))
        outs["mbv_p"].append(pj["vmb"].reshape(batch, seq, MOBA_KV_HEADS, MOBA_HD))
        outs["pool_p"].append(pj["u"].reshape(batch, seq, POOL_WIDTH)[:, seq - POOL_BUF:])
        outs["memk_p"].append(mem_k.reshape(batch, m_tok, MEM_HEADS, MEM_HD))
        outs["memv_p"].append(mem_v.reshape(batch, m_tok, MEM_HEADS, MEM_HD))
        sj = _project(xs, lw, cos_s, sin_s, ndb, False)
        o_lat = _mla_sample(page_table, sj["qcat"].transpose(1, 0, 2), sj["kcat"][:, None, :], cache_mla_ckv,
                            cache_mla_krope, l)
        o_mla = _uv_project(o_lat.transpose(1, 0, 2), lw["wuv"])
        u_s = sj["u"]
        o_pool = _pool_sample(state_pool[l].transpose(1, 0, 2), u_s, lw["pool_w"], lw["pool_scale"], past)
        q_heads = sj["qmb"].reshape(ndb, MOBA_HEADS, MOBA_HD)
        knew = sj["kmb"].reshape(ndb, MOBA_KV_HEADS, MOBA_HD)
        vnew = sj["vmb"].reshape(ndb, MOBA_KV_HEADS, MOBA_HD)
        idx = _moba_gate(page_table, q_heads, cache_moba_k, l, n_full, own_s, topk_s)
        idx = idx[:, :, :topk_s].reshape(ndb, MOBA_HEADS * topk_s)
        o_moba = _moba_sample(page_table, idx, q_heads, knew, vnew, slopes, cache_moba_k, cache_moba_v, l, topk_s,
                              own_s, past).reshape(ndb, MOBA_HEADS * MOBA_HD)
        width = MEM_HEADS * MEM_HD
        o_mem = _mem_sample(sj["qmem"], cache_mem_k[l].reshape(ndb, m_tok, width), cache_mem_v[l].reshape(ndb, m_tok, width), 8)
        xs = _merge(xs, lw, fnorm, (o_mla, o_pool, o_moba, o_mem), ndb, last)
        outs["ckv_s"].append(sj["ckv"][:, None, :])
        outs["kr_s"].append(sj["kr"][:, None, :])
        outs["mbk_s"].append(knew[:, None])
        outs["mbv_s"].append(vnew[:, None])
        outs["pool_s"].append(jnp.concatenate([state_pool[l][:, 1:], u_s[:, None, :]], axis=1))
    st = lambda k: jnp.stack(outs[k])
    return (xp.reshape(batch, seq, D_MODEL), xs.reshape(ndb, t_new, D_MODEL), st("ckv_p"), st("kr_p"), st("mbk_p"),
            st("mbv_p"), st("pool_p"), st("memk_p"), st("memv_p"), st("ckv_s"), st("kr_s"), st("mbk_s"), st("mbv_s"),
            st("pool_s"))
```

```python
import functools

import jax
import jax.numpy as jnp
from jax import lax
from jax.experimental import pallas as pl
from jax.experimental.pallas import tpu as pltpu

F32 = jnp.float32
BF16 = jnp.bfloat16
I32 = jnp.int32

D_MODEL = 1024
PAGE_SIZE = 128
N_BRANCH = 4
BRANCH_WIDTH = 512
MLA_HEADS = 8
MLA_NOPE = 64
MLA_ROPE = 32
MLA_V = 64
MLA_Q_RANK = 384
MLA_KV_RANK = 256
ROPE_BASE = 10000.0
POOL_WINDOWS = (2, 4, 8, 16)
POOL_GROUP = 128
POOL_WIDTH = POOL_GROUP * len(POOL_WINDOWS)
POOL_BUF = max(POOL_WINDOWS) - 1
MOBA_HEADS = 8
MOBA_KV_HEADS = 2
MOBA_REP = MOBA_HEADS // MOBA_KV_HEADS
MOBA_HD = 64
MOBA_BLOCK = 256
MOBA_TOPK = 3
MOBA_PPB = 2
MEM_HEADS = 4
MEM_HD = 128
NORM_EPS = 1e-6
NEG = -1e30
KNOCKOUT = -3.0e38
MASK_SHIFT = 1e30
IN_SIZES = (MLA_Q_RANK, MLA_KV_RANK, MLA_ROPE, POOL_WIDTH, MOBA_HEADS * MOBA_HD, MOBA_KV_HEADS * MOBA_HD,
            MOBA_KV_HEADS * MOBA_HD, MEM_HEADS * MEM_HD, N_BRANCH * BRANCH_WIDTH, N_BRANCH * D_MODEL)
SLOPES = tuple(2.0 ** (-8.0 * (h + 1) / MOBA_HEADS) for h in range(MOBA_HEADS))

LANES = 128
KCAT = MLA_KV_RANK + LANES
HALF = LANES // 2
VMEM_LIMIT = 56 * 1024 * 1024

A_CQ = 0
A_CKV = A_CQ + MLA_Q_RANK
A_U = A_CKV + MLA_KV_RANK
A_QMB = A_U + POOL_WIDTH
A_KMB = A_QMB + MOBA_HEADS * MOBA_HD
A_VMB = A_KMB + LANES
A_QMEM = A_VMB + LANES
A_KRA = A_QMEM + MEM_HEADS * MEM_HD
A_KRB = A_KRA + LANES
A_END = A_KRB + LANES


def _rms(x, g):
    return x * lax.rsqrt(jnp.mean(x * x, axis=-1, keepdims=True) + NORM_EPS) * g


def _bdot(a, b):
    return jnp.dot(a.astype(BF16), b.astype(BF16), preferred_element_type=F32)


def _bdot_nt(a, b):
    return lax.dot_general(a.astype(BF16), b.astype(BF16), (((1,), (1,)), ((), ())), preferred_element_type=F32)


def _sigmoid(x):
    return 1.0 / (1.0 + jnp.exp(-x))


def _const_spec(shape):
    nd = len(shape)
    return pl.BlockSpec(shape, lambda *_: (0,) * nd, pipeline_mode=pl.Buffered(1))


def _params(n_axes=1):
    return pltpu.CompilerParams(dimension_semantics=("arbitrary",) * n_axes, vmem_limit_bytes=VMEM_LIMIT)


def _top_k_mask(gate, colf, k, axis=-1):
    sel = jnp.zeros_like(gate)
    picked = []
    big = jnp.float32(gate.shape[axis])
    for _ in range(k):
        mx = jnp.max(gate, axis=axis, keepdims=True)
        idx = jnp.min(jnp.where(gate == mx, colf, big), axis=axis, keepdims=True)
        hit = colf == idx
        sel = jnp.where(hit, 1.0, sel)
        gate = jnp.where(hit, KNOCKOUT, gate)
        picked.append(idx)
    return sel, picked


N_COMMON = 8


def _proj_kernel(*refs, prompt, npos):
    x_ref, g_ref, wa_ref, qn_ref, kvn_ref, wuq_ref, mh_ref, cos_ref, sin_ref = refs[:9]
    ckv_ref, kr_ref, kcat_ref, qcat_ref, u_ref, kmb_ref, vmb_ref, qmem_ref = refs[9:9 + N_COMMON]
    extra = refs[9 + N_COMMON:]
    tm = x_ref.shape[0]
    h = _rms(x_ref[...], g_ref[...])
    y = _bdot(h, wa_ref[...])
    cos = cos_ref[...]
    sin = sin_ref[...]
    ckv = _rms(y[:, A_CKV:A_U], kvn_ref[...])
    ckv_ref[...] = ckv
    kr = y[:, A_KRA:A_KRB] * cos[:, :LANES] + y[:, A_KRB:A_END] * sin[:, :LANES]
    kr_ref[...] = kr[:, :MLA_ROPE]
    kcat_ref[...] = jnp.concatenate([ckv.astype(BF16), kr.astype(BF16)], axis=-1)
    q1 = _bdot(_rms(y[:, A_CQ:A_CKV], qn_ref[...]), wuq_ref[...])
    n_nope = MLA_HEADS * MLA_NOPE
    n_rope = MLA_HEADS * MLA_ROPE
    roped = q1[:, n_nope:n_nope + n_rope] * cos + q1[:, n_nope + n_rope:] * sin
    for hd in range(MLA_HEADS):
        pair = (hd * MLA_NOPE) // LANES
        quad = (hd * MLA_ROPE) // LANES
        lhs = jnp.concatenate([q1[:, pair * LANES:(pair + 1) * LANES], roped[:, quad * LANES:(quad + 1) * LANES]], axis=-1)
        qcat_ref[hd] = _bdot(lhs, mh_ref[hd]).astype(BF16)
    u_ref[...] = y[:, A_U:A_QMB]
    qs = y[:, A_QMB:A_KMB] * (MOBA_HD ** -0.5)
    kmb = y[:, A_KMB:A_VMB]
    vmb = y[:, A_VMB:A_QMEM]
    kmb_ref[...] = kmb
    vmb_ref[...] = vmb
    qmem_ref[...] = y[:, A_QMEM:A_KRA].astype(BF16)
    if not prompt:
        extra[0][...] = qs.astype(BF16)
        return
    ckvt_ref, qaug_ref, kaug_ref, vt_ref, km_ref = extra
    pos0 = (pl.program_id(0) % npos) * tm
    within = ((pos0 + lax.broadcasted_iota(I32, (tm, 1), 0)) & (MOBA_BLOCK - 1)).astype(F32)
    lane = lax.broadcasted_iota(I32, (tm, LANES), 1)
    low = lane < HALF
    for hd in range(MOBA_HEADS):
        blk = qs[:, (hd // 2) * LANES:(hd // 2 + 1) * LANES]
        if hd % 2:
            blk = pltpu.roll(blk, HALF, 1)
        a = jnp.where(low, blk, 0.0)
        a = jnp.where(lane == HALF, SLOPES[hd], a)
        a = jnp.where(lane == HALF + 1, -SLOPES[hd] * within, a)
        qaug_ref[hd] = a.astype(BF16)
    kparts = []
    for g in range(MOBA_KV_HEADS):
        blk = kmb if g == 0 else pltpu.roll(kmb, HALF, 1)
        a = jnp.where(low, blk, 0.0)
        a = jnp.where(lane == HALF, within, a)
        a = jnp.where(lane == HALF + 1, 1.0, a)
        kparts.append(a)
    kaug_ref[...] = jnp.concatenate(kparts, axis=-1).astype(BF16)
    low1 = lax.broadcasted_iota(I32, (1, LANES), 1) < HALF
    for j in range(tm // MOBA_BLOCK):
        rows = slice(j * MOBA_BLOCK, (j + 1) * MOBA_BLOCK)
        vt_ref[j] = vmb[rows].T.astype(BF16)
        ckvt_ref[j] = ckv[rows].T.astype(BF16)
        mean = jnp.sum(kmb[rows], axis=0, keepdims=True) * (1.0 / MOBA_BLOCK)
        km_ref[j] = jnp.concatenate([jnp.where(low1, mean, 0.0), jnp.where(low1, pltpu.roll(mean, HALF, 1), 0.0)], axis=-1)


def _project(x, lw, cos, sin, tm, prompt):
    n = x.shape[0]
    nt = n // tm
    npos = cos.shape[0] // tm
    row = lambda w: pl.BlockSpec((tm, w), lambda i: (i, 0))
    in_specs = [row(D_MODEL), _const_spec((1, D_MODEL)), _const_spec((D_MODEL, A_END)), _const_spec((1, MLA_Q_RANK)),
                _const_spec((1, MLA_KV_RANK)), _const_spec(lw["wuq"].shape), _const_spec(lw["mh"].shape),
                pl.BlockSpec((tm, 2 * LANES), lambda i: (i % npos, 0)), pl.BlockSpec((tm, 2 * LANES), lambda i: (i % npos, 0))]
    out_shape = [jax.ShapeDtypeStruct((n, MLA_KV_RANK), F32), jax.ShapeDtypeStruct((n, MLA_ROPE), F32),
                 jax.ShapeDtypeStruct((n, KCAT), BF16), jax.ShapeDtypeStruct((MLA_HEADS, n, KCAT), BF16),
                 jax.ShapeDtypeStruct((n, POOL_WIDTH), F32), jax.ShapeDtypeStruct((n, LANES), F32),
                 jax.ShapeDtypeStruct((n, LANES), F32), jax.ShapeDtypeStruct((n, MEM_HEADS * MEM_HD), BF16)]
    out_specs = [row(MLA_KV_RANK), row(MLA_ROPE), row(KCAT), pl.BlockSpec((MLA_HEADS, tm, KCAT), lambda i: (0, i, 0)),
                 row(POOL_WIDTH), row(LANES), row(LANES), row(MEM_HEADS * MEM_HD)]
    names = ["ckv", "kr", "kcat", "qcat", "u", "kmb", "vmb", "qmem"]
    if prompt:
        bpt = tm // MOBA_BLOCK
        nblk = n // MOBA_BLOCK
        blk3 = lambda r, c: pl.BlockSpec((bpt, r, c), lambda i: (i, 0, 0))
        out_shape += [jax.ShapeDtypeStruct((nblk, MLA_KV_RANK, MOBA_BLOCK), BF16),
                      jax.ShapeDtypeStruct((MOBA_HEADS, n, LANES), BF16),
                      jax.ShapeDtypeStruct((n, MOBA_KV_HEADS * LANES), BF16),
                      jax.ShapeDtypeStruct((nblk, LANES, MOBA_BLOCK), BF16),
                      jax.ShapeDtypeStruct((nblk, 1, MOBA_KV_HEADS * LANES), F32)]
        out_specs += [blk3(MLA_KV_RANK, MOBA_BLOCK), pl.BlockSpec((MOBA_HEADS, tm, LANES), lambda i: (0, i, 0)),
                      row(MOBA_KV_HEADS * LANES), blk3(LANES, MOBA_BLOCK), blk3(1, MOBA_KV_HEADS * LANES)]
        names += ["ckvt", "qaug", "kaug", "vt", "km"]
    else:
        out_shape.append(jax.ShapeDtypeStruct((n, MOBA_HEADS * MOBA_HD), BF16))
        out_specs.append(row(MOBA_HEADS * MOBA_HD))
        names.append("qmb")
    outs = pl.pallas_call(
        functools.partial(_proj_kernel, prompt=prompt, npos=npos),
        grid=(nt,), in_specs=in_specs, out_specs=out_specs, out_shape=out_shape,
        compiler_params=_params(), name="proj",
    )(x, lw["ln_g"], lw["wa"], lw["q_norm"], lw["kv_norm"], lw["wuq"], lw["mh"], cos, sin)
    return dict(zip(names, outs))


MLA_CHUNK_HEADS = 2


def _mla_prompt_kernel(q_ref, k_ref, vt_ref, wuv_ref, o_ref, m_ref, l_ref, acc_ref, *, tq, tk, scale):
    i = pl.program_id(1)
    n_chunks = MLA_HEADS // MLA_CHUNK_HEADS
    cw = MLA_CHUNK_HEADS * tq
    sub = tk // MOBA_BLOCK
    jd = (i * tq) // tk

    def q_chunk(c):
        return q_ref[c * MLA_CHUNK_HEADS:(c + 1) * MLA_CHUNK_HEADS].reshape(cw, KCAT)

    def k_tile(j):
        return k_ref[pl.ds(pl.multiple_of(j * tk, tk), tk), :]

    def pv(j, p):
        out = None
        for sb in range(sub):
            t = jnp.dot(vt_ref[j * sub + sb], p[sb * MOBA_BLOCK:(sb + 1) * MOBA_BLOCK], preferred_element_type=F32)
            out = t if out is None else out + t
        return out

    kd = k_tile(jd)
    kpos = jd * tk + lax.broadcasted_iota(I32, (tk, cw), 0)
    qpos = i * tq + (lax.broadcasted_iota(I32, (tk, cw), 1) & (tq - 1))
    causal = kpos <= qpos
    for c in range(n_chunks):
        s = jnp.where(causal, _bdot_nt(kd, q_chunk(c)) * scale, NEG)
        m = jnp.max(s, axis=0, keepdims=True)
        p = jnp.exp(s - m)
        m_ref[c] = m
        l_ref[c] = jnp.sum(p, axis=0, keepdims=True)
        acc_ref[c] = pv(jd, p.astype(BF16))

    def body(j, carry):
        kj = k_tile(j)
        for c in range(n_chunks):
            s = _bdot_nt(kj, q_chunk(c)) * scale
            m_old = m_ref[c]
            m_new = jnp.maximum(m_old, jnp.max(s, axis=0, keepdims=True))
            alpha = jnp.exp(m_old - m_new)
            p = jnp.exp(s - m_new)
            l_ref[c] = alpha * l_ref[c] + jnp.sum(p, axis=0, keepdims=True)
            acc_ref[c] = alpha * acc_ref[c] + pv(j, p.astype(BF16))
            m_ref[c] = m_new
        return carry

    lax.fori_loop(0, jd, body, 0)
    out = jnp.zeros((tq, MLA_HEADS * MLA_V), F32)
    for c in range(n_chunks):
        o_t = acc_ref[c] / l_ref[c]
        for hh in range(MLA_CHUNK_HEADS):
            o_lat = o_t[:, hh * tq:(hh + 1) * tq].T
            out = out + _bdot(o_lat, wuv_ref[c * MLA_CHUNK_HEADS + hh])
    o_ref[...] = out


def _mla_prompt(qcat, kcat, ckvt, wuv, batch, seq, tq, tk):
    nq = seq // tq
    nblk = seq // MOBA_BLOCK
    n_chunks = MLA_HEADS // MLA_CHUNK_HEADS
    cw = MLA_CHUNK_HEADS * tq
    return pl.pallas_call(
        functools.partial(_mla_prompt_kernel, tq=tq, tk=tk, scale=(MLA_NOPE + MLA_ROPE) ** -0.5),
        grid=(batch, nq),
        in_specs=[pl.BlockSpec((MLA_HEADS, tq, KCAT), lambda b, i: (0, b * nq + i, 0)),
                  pl.BlockSpec((seq, KCAT), lambda b, i: (b, 0)),
                  pl.BlockSpec((nblk, MLA_KV_RANK, MOBA_BLOCK), lambda b, i: (b, 0, 0)),
                  _const_spec(wuv.shape)],
        out_specs=pl.BlockSpec((tq, MLA_HEADS * MLA_V), lambda b, i: (b * nq + i, 0)),
        out_shape=jax.ShapeDtypeStruct((batch * seq, MLA_HEADS * MLA_V), F32),
        scratch_shapes=[pltpu.VMEM((n_chunks, 1, cw), F32), pltpu.VMEM((n_chunks, 1, cw), F32),
                        pltpu.VMEM((n_chunks, MLA_KV_RANK, cw), F32)],
        compiler_params=_params(2), name="mla_prompt",
    )(qcat, kcat, ckvt, wuv)


def _moba_prompt_kernel(q_ref, k_ref, vt_ref, km_ref, o_ref, sel_ref, m_ref, l_ref, acc_ref, *, nb, topk):
    tq = MOBA_BLOCK
    ob = pl.program_id(1)
    km = km_ref[...].astype(BF16)
    rowf = lax.broadcasted_iota(I32, (nb, tq), 0).astype(F32)
    is_past = rowf < ob.astype(F32)

    def k_block(g, jb):
        return k_ref[pl.ds(pl.multiple_of(jb * MOBA_BLOCK, MOBA_BLOCK), MOBA_BLOCK), g * LANES:(g + 1) * LANES]

    def v_block(g, jb):
        return vt_ref[jb, g * HALF:(g + 1) * HALF, :]

    causal = lax.broadcasted_iota(I32, (MOBA_BLOCK, tq), 0) <= lax.broadcasted_iota(I32, (MOBA_BLOCK, tq), 1)
    for hd in range(MOBA_HEADS):
        g = hd // MOBA_REP
        q = q_ref[hd]
        gate = jnp.where(is_past, _bdot_nt(km[:, g * LANES:(g + 1) * LANES], q), NEG)
        sel, _ = _top_k_mask(gate, rowf, topk, axis=0)
        sel_ref[hd] = jnp.where(is_past, sel, 0.0)
        s = jnp.where(causal, _bdot_nt(k_block(g, ob), q), NEG)
        m = jnp.max(s, axis=0, keepdims=True)
        p = jnp.exp(s - m)
        m_ref[hd] = m
        l_ref[hd] = jnp.sum(p, axis=0, keepdims=True)
        acc_ref[hd] = jnp.dot(v_block(g, ob), p.astype(BF16), preferred_element_type=F32)

    def body(jb, carry):
        block_gap = ((ob - jb) * MOBA_BLOCK).astype(F32)
        for hd in range(MOBA_HEADS):
            g = hd // MOBA_REP
            s = _bdot_nt(k_block(g, jb), q_ref[hd])
            chosen = sel_ref[hd, pl.ds(jb, 1), :] > 0.5
            far = SLOPES[hd] * block_gap
            m_old = m_ref[hd]
            m_new = jnp.maximum(m_old, jnp.where(chosen, jnp.max(s, axis=0, keepdims=True) - far, NEG))
            p = jnp.exp(s - jnp.where(chosen, m_new + far, MASK_SHIFT))
            alpha = jnp.exp(m_old - m_new)
            l_ref[hd] = alpha * l_ref[hd] + jnp.sum(p, axis=0, keepdims=True)
            acc_ref[hd] = alpha * acc_ref[hd] + jnp.dot(v_block(g, jb), p.astype(BF16), preferred_element_type=F32)
            m_ref[hd] = m_new
        return carry

    lax.fori_loop(0, ob, body, 0)
    for pr in range(MOBA_HEADS // 2):
        oa = acc_ref[2 * pr] / l_ref[2 * pr]
        ob_ = acc_ref[2 * pr + 1] / l_ref[2 * pr + 1]
        o_ref[:, pr * LANES:(pr + 1) * LANES] = jnp.concatenate([oa, ob_], axis=0).T


def _moba_prompt(qaug, kaug, vt, km, batch, seq):
    tq = MOBA_BLOCK
    nb = seq // MOBA_BLOCK
    gw = MOBA_KV_HEADS * LANES
    return pl.pallas_call(
        functools.partial(_moba_prompt_kernel, nb=nb, topk=min(MOBA_TOPK, nb)),
        grid=(batch, nb),
        in_specs=[pl.BlockSpec((MOBA_HEADS, tq, LANES), lambda b, i: (0, b * nb + i, 0)),
                  pl.BlockSpec((seq, gw), lambda b, i: (b, 0)),
                  pl.BlockSpec((nb, LANES, MOBA_BLOCK), lambda b, i: (b, 0, 0)),
                  pl.BlockSpec((nb, gw), lambda b, i: (b, 0))],
        out_specs=pl.BlockSpec((tq, MOBA_HEADS * MOBA_HD), lambda b, i: (b * nb + i, 0)),
        out_shape=jax.ShapeDtypeStruct((batch * seq, MOBA_HEADS * MOBA_HD), F32),
        scratch_shapes=[pltpu.VMEM((MOBA_HEADS, nb, tq), F32), pltpu.VMEM((MOBA_HEADS, 1, tq), F32),
                        pltpu.VMEM((MOBA_HEADS, 1, tq), F32), pltpu.VMEM((MOBA_HEADS, MOBA_HD, tq), F32)],
        compiler_params=_params(2), name="moba_prompt",
    )(qaug, kaug, vt, km)


HALO = 16


def _pool_prompt_kernel(u_ref, halo_ref, w_ref, sc_ref, o_ref, ext_ref, *, tiles_per_seq):
    tm = u_ref.shape[0]
    i = pl.program_id(0) % tiles_per_seq
    u = u_ref[...]
    ext_ref[0:HALO, :] = jnp.where(i > 0, halo_ref[...], 0.0)
    ext_ref[HALO:HALO + tm, :] = u
    pos = i * tm + lax.broadcasted_iota(I32, (tm, 1), 0)
    parts = []
    for g, w in enumerate(POOL_WINDOWS):
        cols = slice(g * POOL_GROUP, (g + 1) * POOL_GROUP)
        tot = u[:, cols]
        for k in range(1, w):
            tot = tot + ext_ref[HALO - k:HALO - k + tm, cols]
        cnt = jnp.minimum(pos + 1, w).astype(F32)
        parts.append(_bdot(tot / cnt - u[:, cols], w_ref[g]))
    o_ref[...] = jnp.concatenate(parts, axis=-1) * sc_ref[...]


def _pool_prompt(u, w_pool, scale, seq, tm):
    n = u.shape[0]
    tps = seq // tm
    hpt = tm // HALO
    return pl.pallas_call(
        functools.partial(_pool_prompt_kernel, tiles_per_seq=tps),
        grid=(n // tm,),
        in_specs=[pl.BlockSpec((tm, POOL_WIDTH), lambda i: (i, 0)),
                  pl.BlockSpec((HALO, POOL_WIDTH), lambda i: (jnp.maximum(i * hpt - 1, 0), 0)),
                  _const_spec(w_pool.shape), _const_spec((1, POOL_WIDTH))],
        out_specs=pl.BlockSpec((tm, POOL_WIDTH), lambda i: (i, 0)),
        out_shape=jax.ShapeDtypeStruct((n, POOL_WIDTH), F32),
        scratch_shapes=[pltpu.VMEM((HALO + tm, POOL_WIDTH), F32)],
        compiler_params=_params(), name="pool_prompt",
    )(u, u, w_pool, scale)


def _pool_sample_kernel(st_ref, u_ref, w_ref, sc_ref, o_ref, *, start_pos):
    u = u_ref[...]
    parts = []
    for g, w in enumerate(POOL_WINDOWS):
        cols = slice(g * POOL_GROUP, (g + 1) * POOL_GROUP)
        tot = u[:, cols]
        for k in range(1, w):
            tot = tot + st_ref[POOL_BUF - k][:, cols]
        parts.append(_bdot(tot / float(min(start_pos + 1, w)) - u[:, cols], w_ref[g]))
    o_ref[...] = jnp.concatenate(parts, axis=-1) * sc_ref[...]


def _pool_sample(state_t, u, w_pool, scale, start_pos):
    n = u.shape[0]
    return pl.pallas_call(
        functools.partial(_pool_sample_kernel, start_pos=start_pos),
        grid=(1,),
        in_specs=[_const_spec(state_t.shape), _const_spec(u.shape), _const_spec(w_pool.shape), _const_spec((1, POOL_WIDTH))],
        out_specs=_const_spec((n, POOL_WIDTH)),
        out_shape=jax.ShapeDtypeStruct((n, POOL_WIDTH), F32),
        compiler_params=_params(), name="pool_sample",
    )(state_t, u, w_pool, scale)


def _mem_kv_kernel(mem_ref, g_ref, w_ref, k_ref, v_ref):
    kv = _bdot(_rms(mem_ref[...], g_ref[...]), w_ref[...])
    half = MEM_HEADS * MEM_HD
    k_ref[...] = kv[:, :half]
    v_ref[...] = kv[:, half:]


def _mem_kv(mem, g, w):
    n = mem.shape[0]
    half = MEM_HEADS * MEM_HD
    return pl.pallas_call(
        _mem_kv_kernel, grid=(1,),
        in_specs=[_const_spec(mem.shape), _const_spec((1, D_MODEL)), _const_spec(w.shape)],
        out_specs=[_const_spec((n, half)), _const_spec((n, half))],
        out_shape=[jax.ShapeDtypeStruct((n, half), F32), jax.ShapeDtypeStruct((n, half), F32)],
        compiler_params=_params(), name="mem_kv",
    )(mem, g, w)


def _mem_prompt_kernel(q_ref, k_ref, v_ref, o_ref):
    q = q_ref[...]
    outs = []
    for hd in range(MEM_HEADS):
        cols = slice(hd * MEM_HD, (hd + 1) * MEM_HD)
        s = _bdot_nt(q[:, cols], k_ref[:, cols]) * (MEM_HD ** -0.5)
        e = jnp.exp(s - jnp.max(s, axis=-1, keepdims=True))
        p = e / jnp.sum(e, axis=-1, keepdims=True)
        outs.append(_bdot(p, v_ref[:, cols]))
    o_ref[...] = jnp.concatenate(outs, axis=-1)


def _mem_prompt(qmem, mem_k, mem_v, seq, tm):
    n = qmem.shape[0]
    tps = seq // tm
    m_tok = mem_k.shape[0] // (n // seq)
    width = MEM_HEADS * MEM_HD
    return pl.pallas_call(
        _mem_prompt_kernel, grid=(n // tm,),
        in_specs=[pl.BlockSpec((tm, width), lambda i: (i, 0)),
                  pl.BlockSpec((m_tok, width), lambda i: (i // tps, 0)),
                  pl.BlockSpec((m_tok, width), lambda i: (i // tps, 0))],
        out_specs=pl.BlockSpec((tm, width), lambda i: (i, 0)),
        out_shape=jax.ShapeDtypeStruct((n, width), F32),
        compiler_params=_params(), name="mem_prompt",
    )(qmem, mem_k, mem_v)


def _mem_sample_kernel(q_ref, k_ref, v_ref, o_ref):
    sb = q_ref.shape[0]
    q = q_ref[...].astype(F32)
    for b in range(sb):
        prod = k_ref[b] * q[b:b + 1, :]
        for hd in range(MEM_HEADS):
            cols = slice(hd * MEM_HD, (hd + 1) * MEM_HD)
            s = jnp.sum(prod[:, cols], axis=-1, keepdims=True) * (MEM_HD ** -0.5)
            e = jnp.exp(s - jnp.max(s, axis=0, keepdims=True))
            o = jnp.sum(e * v_ref[b, :, cols], axis=0, keepdims=True) / jnp.sum(e, axis=0, keepdims=True)
            o_ref[b:b + 1, cols] = o


def _mem_sample(qmem, mem_k, mem_v, sb):
    n, m_tok, width = mem_k.shape
    return pl.pallas_call(
        _mem_sample_kernel, grid=(n // sb,),
        in_specs=[pl.BlockSpec((sb, width), lambda i: (i, 0)),
                  pl.BlockSpec((sb, m_tok, width), lambda i: (i, 0, 0)),
                  pl.BlockSpec((sb, m_tok, width), lambda i: (i, 0, 0))],
        out_specs=pl.BlockSpec((sb, width), lambda i: (i, 0)),
        out_shape=jax.ShapeDtypeStruct((n, width), F32),
        compiler_params=_params(), name="mem_sample",
    )(qmem, mem_k, mem_v)


def _merge_kernel(x_ref, g_ref, wz_ref, wgl_ref, wbr_ref, wout_ref, fn_ref, b0_ref, b1_ref, b2_ref, b3_ref, o_ref, *, final):
    x = x_ref[...]
    h = _rms(x, g_ref[...]).astype(BF16)
    acc = jnp.zeros(x.shape, F32)
    for b, br in enumerate((b0_ref, b1_ref, b2_ref, b3_ref)):
        z = jnp.dot(h, wz_ref[:, b * BRANCH_WIDTH:(b + 1) * BRANCH_WIDTH], preferred_element_type=F32)
        gl = jnp.dot(h, wgl_ref[:, b * D_MODEL:(b + 1) * D_MODEL], preferred_element_type=F32)
        o = br[...] * (z * _sigmoid(z))
        acc = acc + _sigmoid(gl) * _bdot(o, wbr_ref[b])
    y = x + _bdot(acc, wout_ref[...])
    if final:
        y = _rms(y, fn_ref[...])
    o_ref[...] = y


def _merge(x, lw, final_norm, branches, tm, final):
    n = x.shape[0]
    row = lambda w: pl.BlockSpec((tm, w), lambda i: (i, 0))
    return pl.pallas_call(
        functools.partial(_merge_kernel, final=final),
        grid=(n // tm,),
        in_specs=[row(D_MODEL), _const_spec((1, D_MODEL)), _const_spec(lw["wz"].shape), _const_spec(lw["wgl"].shape),
                  _const_spec(lw["wbr"].shape), _const_spec(lw["wout"].shape), _const_spec((1, D_MODEL))]
                 + [row(BRANCH_WIDTH)] * N_BRANCH,
        out_specs=row(D_MODEL),
        out_shape=jax.ShapeDtypeStruct((n, D_MODEL), F32),
        compiler_params=_params(), name="merge",
    )(x, lw["ln_g"], lw["wz"], lw["wgl"], lw["wbr"], lw["wout"], final_norm, *branches)


def _chunking(n_pages):
    ch = 16 if n_pages % 32 == 0 else n_pages // 2
    assert ch >= 1 and n_pages % (2 * ch) == 0, n_pages
    return ch, n_pages // ch


def _mla_sample_kernel(pt_ref, q_ref, kn_ref, ckv_hbm, krt_hbm, o_ref, kbuf, rbuf, sem, *, layer, ch, nch, scale):
    b = pl.program_id(0)
    nb = pl.num_programs(0)

    def copies(bb, c, slot):
        out = []
        for p in range(ch):
            page = pt_ref[bb, c * ch + p]
            out.append(pltpu.make_async_copy(ckv_hbm.at[layer, page], kbuf.at[slot, pl.ds(p * PAGE_SIZE, PAGE_SIZE)],
                                             sem.at[0, slot]))
            out.append(pltpu.make_async_copy(krt_hbm.at[layer, page], rbuf.at[slot, p], sem.at[1, slot]))
        return out

    @pl.when(b == 0)
    def _():
        for cp in copies(0, 0, 0):
            cp.start()

    q = q_ref[0]
    q_lat = q[:, :MLA_KV_RANK]
    q_rope = q[:, MLA_KV_RANK:MLA_KV_RANK + MLA_ROPE]
    kn = kn_ref[0]
    m = jnp.sum(q.astype(F32) * kn.astype(F32), axis=-1, keepdims=True) * scale
    l = jnp.ones_like(m)
    acc = jnp.broadcast_to(kn[:, :MLA_KV_RANK].astype(F32), (MLA_HEADS, MLA_KV_RANK))
    for c in range(nch):
        slot = c % 2
        if c + 1 < nch:
            for cp in copies(b, c + 1, 1 - slot):
                cp.start()
        else:
            @pl.when(b + 1 < nb)
            def _():
                for cp in copies(b + 1, 0, 1 - slot):
                    cp.start()
        for cp in copies(b, c, slot):
            cp.wait()
        kb = kbuf[slot].astype(BF16)
        s_rope = jnp.concatenate([_bdot(q_rope, rbuf[slot, p]) for p in range(ch)], axis=-1)
        s = (_bdot_nt(q_lat, kb) + s_rope) * scale
        m_new = jnp.maximum(m, jnp.max(s, axis=-1, keepdims=True))
        alpha = jnp.exp(m - m_new)
        p = jnp.exp(s - m_new)
        l = alpha * l + jnp.sum(p, axis=-1, keepdims=True)
        acc = alpha * acc + jnp.dot(p.astype(BF16), kb, preferred_element_type=F32)
        m = m_new
    o_ref[0] = acc / l


def _mla_sample(page_table, q, knew, cache_ckv, cache_krt, layer):
    ndb, n_pages = page_table.shape
    ch, nch = _chunking(n_pages)
    rows = ch * PAGE_SIZE
    return pl.pallas_call(
        functools.partial(_mla_sample_kernel, layer=layer, ch=ch, nch=nch, scale=(MLA_NOPE + MLA_ROPE) ** -0.5),
        grid_spec=pltpu.PrefetchScalarGridSpec(
            num_scalar_prefetch=1, grid=(ndb,),
            in_specs=[pl.BlockSpec((1, MLA_HEADS, KCAT), lambda b, pt: (b, 0, 0)),
                      pl.BlockSpec((1, 1, KCAT), lambda b, pt: (b, 0, 0)),
                      pl.BlockSpec(memory_space=pl.ANY), pl.BlockSpec(memory_space=pl.ANY)],
            out_specs=pl.BlockSpec((1, MLA_HEADS, MLA_KV_RANK), lambda b, pt: (b, 0, 0)),
            scratch_shapes=[pltpu.VMEM((2, rows, MLA_KV_RANK), F32), pltpu.VMEM((2, ch, MLA_ROPE, PAGE_SIZE), F32),
                            pltpu.SemaphoreType.DMA((2, 2))]),
        out_shape=jax.ShapeDtypeStruct((ndb, MLA_HEADS, MLA_KV_RANK), F32),
        compiler_params=_params(), name="mla_sample",
    )(page_table, q, knew, cache_ckv, cache_krt)


def _uv_kernel(o_ref, wuv_ref, y_ref):
    out = jnp.zeros(y_ref.shape, F32)
    for hd in range(MLA_HEADS):
        out = out + _bdot(o_ref[hd], wuv_ref[hd])
    y_ref[...] = out


def _uv_project(o_lat, wuv):
    n = o_lat.shape[1]
    return pl.pallas_call(
        _uv_kernel, grid=(1,),
        in_specs=[_const_spec(o_lat.shape), _const_spec(wuv.shape)],
        out_specs=_const_spec((n, MLA_HEADS * MLA_V)),
        out_shape=jax.ShapeDtypeStruct((n, MLA_HEADS * MLA_V), F32),
        compiler_params=_params(), name="mla_uv",
    )(o_lat, wuv)


def _moba_gate_kernel(pt_ref, q_ref, k_hbm, idx_ref, kbuf, km_ref, sem, *, layer, ch, nch, own, topk):
    b = pl.program_id(0)
    nb = pl.num_programs(0)

    def copies(bb, c, slot):
        return [pltpu.make_async_copy(k_hbm.at[layer, pt_ref[bb, c * ch + p]], kbuf.at[slot, p], sem.at[slot])
                for p in range(ch)]

    @pl.when(b == 0)
    def _():
        for cp in copies(0, 0, 0):
            cp.start()

    bpc = ch // MOBA_PPB
    for c in range(nch):
        slot = c % 2
        if c + 1 < nch:
            for cp in copies(b, c + 1, 1 - slot):
                cp.start()
        else:
            @pl.when(b + 1 < nb)
            def _():
                for cp in copies(b + 1, 0, 1 - slot):
                    cp.start()
        for cp in copies(b, c, slot):
            cp.wait()
        for bi in range(bpc):
            n = c * bpc + bi
            for g in range(MOBA_KV_HEADS):
                tot = kbuf[slot, bi * MOBA_PPB, g]
                for j in range(1, MOBA_PPB):
                    tot = tot + kbuf[slot, bi * MOBA_PPB + j, g]
                km_ref[g, :, n:n + 1] = jnp.sum(tot, axis=-1, keepdims=True) * (1.0 / MOBA_BLOCK)
    q = q_ref[0]
    nblk = km_ref.shape[2]
    g0 = _bdot(q, km_ref[0])
    g1 = _bdot(q, km_ref[1])
    head = lax.broadcasted_iota(I32, (MOBA_HEADS, nblk), 0)
    colf = lax.broadcasted_iota(I32, (MOBA_HEADS, nblk), 1).astype(F32)
    gate = jnp.where(head < MOBA_REP, g0, g1)
    gate = jnp.where(colf < float(own), gate, NEG)
    _, picked = _top_k_mask(gate, colf, topk)
    lane = lax.broadcasted_iota(I32, (MOBA_HEADS, LANES), 1)
    out = jnp.zeros((MOBA_HEADS, LANES), F32)
    for t, idx in enumerate(picked):
        out = jnp.where(lane == t, idx, out)
    idx_ref[0] = out.astype(I32)


def _moba_gate(page_table, q, cache_kt, layer, n_full, own, topk):
    ndb = page_table.shape[0]
    ch, nch = _chunking(n_full * MOBA_PPB)
    return pl.pallas_call(
        functools.partial(_moba_gate_kernel, layer=layer, ch=ch, nch=nch, own=own, topk=topk),
        grid_spec=pltpu.PrefetchScalarGridSpec(
            num_scalar_prefetch=1, grid=(ndb,),
            in_specs=[pl.BlockSpec((1, MOBA_HEADS, MOBA_HD), lambda b, pt: (b, 0, 0)),
                      pl.BlockSpec(memory_space=pl.ANY)],
            out_specs=pl.BlockSpec((1, MOBA_HEADS, LANES), lambda b, pt: (b, 0, 0)),
            scratch_shapes=[pltpu.VMEM((2, ch, MOBA_KV_HEADS, MOBA_HD, PAGE_SIZE), F32),
                            pltpu.VMEM((MOBA_KV_HEADS, MOBA_HD, n_full), F32),
                            pltpu.SemaphoreType.DMA((2,))]),
        out_shape=jax.ShapeDtypeStruct((ndb, MOBA_HEADS, LANES), I32),
        compiler_params=_params(), name="moba_gate",
    )(page_table, q, cache_kt)


def _moba_sample_kernel(pt_ref, idx_ref, qt_ref, knt_ref, vnt_ref, k_hbm, v_hbm, ot_ref, kbuf, vbuf, sem,
                        *, layer, topk, own, q_pos):
    b = pl.program_id(0)
    nb = pl.num_programs(0)

    def copies(bb, slot):
        out = []
        for hd in range(MOBA_HEADS):
            g = hd // MOBA_REP
            for t in range(topk):
                blk = idx_ref[bb, hd * topk + t]
                for j in range(MOBA_PPB):
                    page = pt_ref[bb, blk * MOBA_PPB + j]
                    dst = (hd * topk + t) * MOBA_PPB + j
                    out.append(pltpu.make_async_copy(k_hbm.at[layer, page, g], kbuf.at[slot, dst], sem.at[0, slot]))
                    out.append(pltpu.make_async_copy(v_hbm.at[layer, page, g], vbuf.at[slot, dst], sem.at[1, slot]))
        return out

    slot = b % 2

    @pl.when(b == 0)
    def _():
        for cp in copies(0, 0):
            cp.start()

    @pl.when(b + 1 < nb)
    def _():
        for cp in copies(b + 1, 1 - slot):
            cp.start()

    for cp in copies(b, slot):
        cp.wait()

    qt = qt_ref[0]
    knt = knt_ref[0]
    vnt = vnt_ref[0]
    lane = lax.broadcasted_iota(I32, (1, PAGE_SIZE), 1)
    for hd in range(MOBA_HEADS):
        g = hd // MOBA_REP
        qcol = qt[:, hd:hd + 1]
        pieces = []
        for t in range(topk):
            blk = idx_ref[b, hd * topk + t]
            for j in range(MOBA_PPB):
                kt = kbuf[slot, (hd * topk + t) * MOBA_PPB + j]
                dist = (q_pos - (blk * MOBA_BLOCK + j * PAGE_SIZE + lane)).astype(F32)
                s = jnp.sum(kt * qcol, axis=0, keepdims=True) - SLOPES[hd] * dist
                pieces.append(jnp.where(blk < own, s, NEG))
        s_self = jnp.sum(qcol * knt[:, g:g + 1], axis=0, keepdims=True)
        smax = pieces[0]
        for s in pieces[1:]:
            smax = jnp.maximum(smax, s)
        m = jnp.maximum(jnp.max(smax, axis=-1, keepdims=True), s_self)
        p_self = jnp.exp(s_self - m)
        den = p_self
        wsum = jnp.zeros((MOBA_HD, PAGE_SIZE), F32)
        for pi, s in enumerate(pieces):
            p = jnp.exp(s - m)
            den = den + jnp.sum(p, axis=-1, keepdims=True)
            wsum = wsum + vbuf[slot, hd * topk * MOBA_PPB + pi] * p
        o = (jnp.sum(wsum, axis=-1, keepdims=True) + p_self * vnt[:, g:g + 1]) / den
        ot_ref[0, :, hd:hd + 1] = o


def _moba_sample(page_table, idx, qt, knt, vnt, cache_kt, cache_vt, layer, topk, own, q_pos):
    ndb = page_table.shape[0]
    n_buf = MOBA_HEADS * topk * MOBA_PPB
    col = lambda w: pl.BlockSpec((1, MOBA_HD, w), lambda b, pt, ix: (b, 0, 0))
    return pl.pallas_call(
        functools.partial(_moba_sample_kernel, layer=layer, topk=topk, own=own, q_pos=q_pos),
        grid_spec=pltpu.PrefetchScalarGridSpec(
            num_scalar_prefetch=2, grid=(ndb,),
            in_specs=[col(MOBA_HEADS), col(MOBA_KV_HEADS), col(MOBA_KV_HEADS),
                      pl.BlockSpec(memory_space=pl.ANY), pl.BlockSpec(memory_space=pl.ANY)],
            out_specs=col(MOBA_HEADS),
            scratch_shapes=[pltpu.VMEM((2, n_buf, MOBA_HD, PAGE_SIZE), F32),
                            pltpu.VMEM((2, n_buf, MOBA_HD, PAGE_SIZE), F32),
                            pltpu.SemaphoreType.DMA((2, 2))]),
        out_shape=jax.ShapeDtypeStruct((ndb, MOBA_HD, MOBA_HEADS), F32),
        compiler_params=_params(), name="moba_sample",
    )(page_table, idx, qt, knt, vnt, cache_kt, cache_vt)


def _rope_tables(pos):
    half = MLA_ROPE // 2
    inv = ROPE_BASE ** (-jnp.arange(half, dtype=F32) / half)
    ang = pos.astype(F32)[:, None] * inv[None, :]
    cos, sin = jnp.cos(ang), jnp.sin(ang)
    reps = 2 * LANES // MLA_ROPE
    return jnp.tile(jnp.concatenate([cos, cos], axis=-1), (1, reps)), jnp.tile(jnp.concatenate([-sin, sin], axis=-1), (1, reps))


def _pad_lanes(w):
    return jnp.pad(w, ((0, 0), (0, LANES - w.shape[1])))


def _layer_weights(l, ln_g, w_in, mla_q_norm, mla_w_uq, mla_kv_norm, mla_w_uk, mla_w_uv, pool_w, pool_scale,
                   mem_norm, w_mem_kv, w_branch, w_out):
    half = MLA_ROPE // 2
    swap = jnp.concatenate([jnp.arange(half, MLA_ROPE), jnp.arange(half)])
    offs = [0]
    for s in IN_SIZES:
        offs.append(offs[-1] + s)
    c_q, c_kv, k_rope, u, q_mb, k_mb, v_mb, q_mem, z, gl = [w_in[l][:, offs[i]:offs[i + 1]] for i in range(len(IN_SIZES))]
    wa = jnp.concatenate([c_q, c_kv, u, q_mb, k_mb, v_mb, q_mem, _pad_lanes(k_rope), _pad_lanes(k_rope[:, swap])], axis=1)
    uq = mla_w_uq[l]
    qr = uq.shape[0]
    wuq = jnp.concatenate([uq[:, :, :MLA_NOPE].reshape(qr, -1), uq[:, :, MLA_NOPE:].reshape(qr, -1),
                           uq[:, :, MLA_NOPE:][:, :, swap].reshape(qr, -1)], axis=1)
    mh = jnp.zeros((MLA_HEADS, 2 * LANES, KCAT), F32)
    eye = jnp.eye(MLA_ROPE, dtype=F32)
    for hd in range(MLA_HEADS):
        r0 = (hd * MLA_NOPE) % LANES
        mh = mh.at[hd, r0:r0 + MLA_NOPE, :MLA_KV_RANK].set(mla_w_uk[l][:, hd, :].T)
        r1 = LANES + (hd * MLA_ROPE) % LANES
        mh = mh.at[hd, r1:r1 + MLA_ROPE, MLA_KV_RANK:MLA_KV_RANK + MLA_ROPE].set(eye)
    wuv = jnp.zeros((MLA_HEADS, MLA_KV_RANK, MLA_HEADS * MLA_V), F32)
    for hd in range(MLA_HEADS):
        wuv = wuv.at[hd, :, hd * MLA_V:(hd + 1) * MLA_V].set(mla_w_uv[l][:, hd, :])
    return dict(
        ln_g=ln_g[l][None, :], wa=wa.astype(BF16), q_norm=mla_q_norm[l][None, :], kv_norm=mla_kv_norm[l][None, :],
        wuq=wuq.astype(BF16), mh=mh.astype(BF16), wuv=wuv.astype(BF16), wz=z.astype(BF16), wgl=gl.astype(BF16),
        wbr=w_branch[l].astype(BF16), wout=w_out[l].astype(BF16), pool_w=pool_w[l].astype(BF16),
        pool_scale=pool_scale[l][None, :], mem_norm=mem_norm[l][None, :], w_mem_kv=w_mem_kv[l].astype(BF16))


def kernel(x_prompt, x_sample, cache_mla_ckv, cache_mla_krope, cache_moba_k, cache_moba_v, state_pool, cache_mem_k, cache_mem_v, page_table, mem_prompt, ln_g, w_in, mla_q_norm, mla_w_uq, mla_kv_norm, mla_w_uk, mla_w_uv, pool_w, pool_scale, mem_norm, w_mem_kv, w_branch, w_out, final_norm):
    batch, seq, _ = x_prompt.shape
    ndb, t_new, _ = x_sample.shape
    depth = ln_g.shape[0]
    n_pages = page_table.shape[1]
    past = n_pages * PAGE_SIZE
    m_tok = mem_prompt.shape[1]
    assert t_new == 1, "one new token per sample"
    assert past % MOBA_BLOCK == 0, "no partially filled cached MoBA block"
    assert seq % MOBA_BLOCK == 0 and ndb % 8 == 0 and MOBA_PPB * PAGE_SIZE == MOBA_BLOCK
    tm = MOBA_BLOCK
    tq = 128
    tk = min(512, seq)
    n_full = past // MOBA_BLOCK
    own_s = past // MOBA_BLOCK
    topk_s = min(MOBA_TOPK, n_full)
    cos_p, sin_p = _rope_tables(jnp.arange(seq))
    cos_s, sin_s = _rope_tables(jnp.full((ndb,), past))
    fnorm = final_norm[None, :]
    cache_krt = cache_mla_krope.transpose(0, 1, 3, 2)
    cache_kt = cache_moba_k.transpose(0, 1, 3, 4, 2)
    cache_vt = cache_moba_v.transpose(0, 1, 3, 4, 2)

    xp = x_prompt.reshape(batch * seq, D_MODEL)
    xs = x_sample.reshape(ndb, D_MODEL)
    mem_flat = mem_prompt.reshape(batch * m_tok, D_MODEL)
    outs = {k: [] for k in ("ckv_p", "kr_p", "mbk_p", "mbv_p", "pool_p", "memk_p", "memv_p",
                            "ckv_s", "kr_s", "mbk_s", "mbv_s", "pool_s")}
    for l in range(depth):
        lw = _layer_weights(l, ln_g, w_in, mla_q_norm, mla_w_uq, mla_kv_norm, mla_w_uk, mla_w_uv, pool_w, pool_scale,
                            mem_norm, w_mem_kv, w_branch, w_out)
        last = l == depth - 1
        pj = _project(xp, lw, cos_p, sin_p, tm, True)
        o_mla = _mla_prompt(pj["qcat"], pj["kcat"], pj["ckvt"], lw["wuv"], batch, seq, tq, tk)
        o_pool = _pool_prompt(pj["u"], lw["pool_w"], lw["pool_scale"], seq, tm)
        km = pj["km"].reshape(batch * seq // MOBA_BLOCK, MOBA_KV_HEADS * LANES)
        o_moba = _moba_prompt(pj["qaug"], pj["kaug"], pj["vt"], km, batch, seq)
        mem_k, mem_v = _mem_kv(mem_flat, lw["mem_norm"], lw["w_mem_kv"])
        o_mem = _mem_prompt(pj["qmem"], mem_k, mem_v, seq, tm)
        xp = _merge(xp, lw, fnorm, (o_mla, o_pool, o_moba, o_mem), tm, last)
        outs["ckv_p"].append(pj["ckv"].reshape(batch, seq, MLA_KV_RANK))
        outs["kr_p"].append(pj["kr"].reshape(batch, seq, MLA_ROPE))
        outs["mbk_p"].append(pj["kmb"].reshape(batch, seq, MOBA_KV_HEADS, MOBA_HD))
        outs["mbv_p"].append(pj["vmb"].reshape(batch, seq, MOBA_KV_HEADS, MOBA_HD))
        outs["pool_p"].append(pj["u"].reshape(batch, seq, POOL_WIDTH)[:, seq - POOL_BUF:])
        outs["memk_p"].append(mem_k.reshape(batch, m_tok, MEM_HEADS, MEM_HD))
        outs["memv_p"].append(mem_v.reshape(batch, m_tok, MEM_HEADS, MEM_HD))
        sj = _project(xs, lw, cos_s, sin_s, ndb, False)
        o_lat = _mla_sample(page_table, sj["qcat"].transpose(1, 0, 2), sj["kcat"][:, None, :], cache_mla_ckv,
                            cache_krt, l)
        o_mla = _uv_project(o_lat.transpose(1, 0, 2), lw["wuv"])
        u_s = sj["u"]
        o_pool = _pool_sample(state_pool[l].transpose(1, 0, 2), u_s, lw["pool_w"], lw["pool_scale"], past)
        q_heads = sj["qmb"].reshape(ndb, MOBA_HEADS, MOBA_HD)
        knew = sj["kmb"].reshape(ndb, MOBA_KV_HEADS, MOBA_HD)
        vnew = sj["vmb"].reshape(ndb, MOBA_KV_HEADS, MOBA_HD)
        idx = _moba_gate(page_table, q_heads, cache_kt, l, n_full, own_s, topk_s)
        idx = idx[:, :, :topk_s].reshape(ndb, MOBA_HEADS * topk_s)
        o_moba_t = _moba_sample(page_table, idx, q_heads.astype(F32).transpose(0, 2, 1), knew.transpose(0, 2, 1),
                                vnew.transpose(0, 2, 1), cache_kt, cache_vt, l, topk_s, own_s, past)
        o_moba = o_moba_t.transpose(0, 2, 1).reshape(ndb, MOBA_HEADS * MOBA_HD)
        width = MEM_HEADS * MEM_HD
        o_mem = _mem_sample(sj["qmem"], cache_mem_k[l].reshape(ndb, m_tok, width), cache_mem_v[l].reshape(ndb, m_tok, width), 8)
        xs = _merge(xs, lw, fnorm, (o_mla, o_pool, o_moba, o_mem), ndb, last)
        outs["ckv_s"].append(sj["ckv"][:, None, :])
        outs["kr_s"].append(sj["kr"][:, None, :])
        outs["mbk_s"].append(knew[:, None])
        outs["mbv_s"].append(vnew[:, None])
        outs["pool_s"].append(jnp.concatenate([state_pool[l][:, 1:], u_s[:, None, :]], axis=1))
    st = lambda k: jnp.stack(outs[k])
    return (xp.reshape(batch, seq, D_MODEL), xs.reshape(ndb, t_new, D_MODEL), st("ckv_p"), st("kr_p"), st("mbk_p"),
            st("mbv_p"), st("pool_p"), st("memk_p"), st("memv_p"), st("ckv_s"), st("kr_s"), st("mbk_s"), st("mbv_s"),
            st("pool_s"))
```

```python
import functools

import jax
import jax.numpy as jnp
import numpy as np
from jax import lax
from jax.experimental import pallas as pl
from jax.experimental.pallas import tpu as pltpu

F32 = jnp.float32
BF16 = jnp.bfloat16
I32 = jnp.int32

D_MODEL = 1024
PAGE_SIZE = 128
N_BRANCH = 4
BRANCH_WIDTH = 512
MLA_HEADS = 8
MLA_NOPE = 64
MLA_ROPE = 32
MLA_V = 64
MLA_Q_RANK = 384
MLA_KV_RANK = 256
ROPE_BASE = 10000.0
POOL_WINDOWS = (2, 4, 8, 16)
POOL_GROUP = 128
POOL_WIDTH = POOL_GROUP * len(POOL_WINDOWS)
POOL_BUF = max(POOL_WINDOWS) - 1
MOBA_HEADS = 8
MOBA_KV_HEADS = 2
MOBA_REP = MOBA_HEADS // MOBA_KV_HEADS
MOBA_HD = 64
MOBA_BLOCK = 256
MOBA_TOPK = 3
MOBA_PPB = 2
MEM_HEADS = 4
MEM_HD = 128
NORM_EPS = 1e-6
NEG = -1e30
KNOCKOUT = -3.0e38
MASK_SHIFT = 1e30
IN_SIZES = (MLA_Q_RANK, MLA_KV_RANK, MLA_ROPE, POOL_WIDTH, MOBA_HEADS * MOBA_HD, MOBA_KV_HEADS * MOBA_HD,
            MOBA_KV_HEADS * MOBA_HD, MEM_HEADS * MEM_HD, N_BRANCH * BRANCH_WIDTH, N_BRANCH * D_MODEL)
SLOPES = tuple(2.0 ** (-8.0 * (h + 1) / MOBA_HEADS) for h in range(MOBA_HEADS))

LANES = 128
KCAT = MLA_KV_RANK + LANES
HALF = LANES // 2
VMEM_LIMIT = 56 * 1024 * 1024

A_CQ = 0
A_CKV = A_CQ + MLA_Q_RANK
A_U = A_CKV + MLA_KV_RANK
A_QMB = A_U + POOL_WIDTH
A_KMB = A_QMB + MOBA_HEADS * MOBA_HD
A_VMB = A_KMB + LANES
A_QMEM = A_VMB + LANES
A_KRA = A_QMEM + MEM_HEADS * MEM_HD
A_KRB = A_KRA + LANES
A_END = A_KRB + LANES


def _rms(x, g):
    return x * lax.rsqrt(jnp.mean(x * x, axis=-1, keepdims=True) + NORM_EPS) * g


def _bdot(a, b):
    return jnp.dot(a.astype(BF16), b.astype(BF16), preferred_element_type=F32)


def _bdot_nt(a, b):
    return lax.dot_general(a.astype(BF16), b.astype(BF16), (((1,), (1,)), ((), ())), preferred_element_type=F32)


def _sigmoid(x):
    return 1.0 / (1.0 + jnp.exp(-x))


def _const_spec(shape):
    nd = len(shape)
    return pl.BlockSpec(shape, lambda *_: (0,) * nd, pipeline_mode=pl.Buffered(1))


def _params(n_axes=1):
    return pltpu.CompilerParams(dimension_semantics=("arbitrary",) * n_axes, vmem_limit_bytes=VMEM_LIMIT)


def _top_k_mask(gate, colf, k, axis=-1):
    sel = jnp.zeros_like(gate)
    picked = []
    big = jnp.float32(gate.shape[axis])
    for _ in range(k):
        mx = jnp.max(gate, axis=axis, keepdims=True)
        idx = jnp.min(jnp.where(gate == mx, colf, big), axis=axis, keepdims=True)
        hit = colf == idx
        sel = jnp.where(hit, 1.0, sel)
        gate = jnp.where(hit, KNOCKOUT, gate)
        picked.append(idx)
    return sel, picked


N_COMMON = 8


def _proj_kernel(*refs, prompt, npos):
    x_ref, g_ref, wa_ref, qn_ref, kvn_ref, wuq_ref, mh_ref, cos_ref, sin_ref = refs[:9]
    ckv_ref, kr_ref, kcat_ref, qcat_ref, u_ref, kmb_ref, vmb_ref, qmem_ref = refs[9:9 + N_COMMON]
    extra = refs[9 + N_COMMON:]
    tm = x_ref.shape[0]
    h = _rms(x_ref[...], g_ref[...])
    y = _bdot(h, wa_ref[...])
    cos = cos_ref[...]
    sin = sin_ref[...]
    ckv = _rms(y[:, A_CKV:A_U], kvn_ref[...])
    ckv_ref[...] = ckv
    kr = y[:, A_KRA:A_KRB] * cos[:, :LANES] + y[:, A_KRB:A_END] * sin[:, :LANES]
    kr_ref[...] = kr[:, :MLA_ROPE]
    kcat_ref[...] = jnp.concatenate([ckv.astype(BF16), kr.astype(BF16)], axis=-1)
    q1 = _bdot(_rms(y[:, A_CQ:A_CKV], qn_ref[...]), wuq_ref[...])
    n_nope = MLA_HEADS * MLA_NOPE
    n_rope = MLA_HEADS * MLA_ROPE
    roped = q1[:, n_nope:n_nope + n_rope] * cos + q1[:, n_nope + n_rope:] * sin
    for hd in range(MLA_HEADS):
        pair = (hd * MLA_NOPE) // LANES
        quad = (hd * MLA_ROPE) // LANES
        lhs = jnp.concatenate([q1[:, pair * LANES:(pair + 1) * LANES], roped[:, quad * LANES:(quad + 1) * LANES]], axis=-1)
        qcat_ref[hd] = _bdot(lhs, mh_ref[hd]).astype(BF16)
    u_ref[...] = y[:, A_U:A_QMB]
    qs = y[:, A_QMB:A_KMB] * (MOBA_HD ** -0.5)
    kmb = y[:, A_KMB:A_VMB]
    vmb = y[:, A_VMB:A_QMEM]
    kmb_ref[...] = kmb
    vmb_ref[...] = vmb
    qmem_ref[...] = y[:, A_QMEM:A_KRA].astype(BF16)
    if not prompt:
        extra[0][...] = qs.astype(BF16)
        return
    ckvt_ref, qaug_ref, kaug_ref, vt_ref, km_ref = extra
    pos0 = (pl.program_id(0) % npos) * tm
    within = ((pos0 + lax.broadcasted_iota(I32, (tm, 1), 0)) & (MOBA_BLOCK - 1)).astype(F32)
    lane = lax.broadcasted_iota(I32, (tm, LANES), 1)
    low = lane < HALF
    for hd in range(MOBA_HEADS):
        blk = qs[:, (hd // 2) * LANES:(hd // 2 + 1) * LANES]
        if hd % 2:
            blk = pltpu.roll(blk, HALF, 1)
        a = jnp.where(low, blk, 0.0)
        a = jnp.where(lane == HALF, SLOPES[hd], a)
        a = jnp.where(lane == HALF + 1, -SLOPES[hd] * within, a)
        qaug_ref[hd] = a.astype(BF16)
    kparts = []
    for g in range(MOBA_KV_HEADS):
        blk = kmb if g == 0 else pltpu.roll(kmb, HALF, 1)
        a = jnp.where(low, blk, 0.0)
        a = jnp.where(lane == HALF, within, a)
        a = jnp.where(lane == HALF + 1, 1.0, a)
        kparts.append(a)
    kaug_ref[...] = jnp.concatenate(kparts, axis=-1).astype(BF16)
    low1 = lax.broadcasted_iota(I32, (1, LANES), 1) < HALF
    for j in range(tm // MOBA_BLOCK):
        rows = slice(j * MOBA_BLOCK, (j + 1) * MOBA_BLOCK)
        vt_ref[j] = vmb[rows].T.astype(BF16)
        ckvt_ref[j] = ckv[rows].T.astype(BF16)
        mean = jnp.sum(kmb[rows], axis=0, keepdims=True) * (1.0 / MOBA_BLOCK)
        km_ref[j] = jnp.concatenate([jnp.where(low1, mean, 0.0), jnp.where(low1, pltpu.roll(mean, HALF, 1), 0.0)], axis=-1)


def _project(x, lw, cos, sin, tm, prompt):
    n = x.shape[0]
    nt = n // tm
    npos = cos.shape[0] // tm
    row = lambda w: pl.BlockSpec((tm, w), lambda i: (i, 0))
    in_specs = [row(D_MODEL), _const_spec((1, D_MODEL)), _const_spec((D_MODEL, A_END)), _const_spec((1, MLA_Q_RANK)),
                _const_spec((1, MLA_KV_RANK)), _const_spec(lw["wuq"].shape), _const_spec(lw["mh"].shape),
                pl.BlockSpec((tm, 2 * LANES), lambda i: (i % npos, 0)), pl.BlockSpec((tm, 2 * LANES), lambda i: (i % npos, 0))]
    out_shape = [jax.ShapeDtypeStruct((n, MLA_KV_RANK), F32), jax.ShapeDtypeStruct((n, MLA_ROPE), F32),
                 jax.ShapeDtypeStruct((n, KCAT), BF16), jax.ShapeDtypeStruct((MLA_HEADS, n, KCAT), BF16),
                 jax.ShapeDtypeStruct((n, POOL_WIDTH), F32), jax.ShapeDtypeStruct((n, LANES), F32),
                 jax.ShapeDtypeStruct((n, LANES), F32), jax.ShapeDtypeStruct((n, MEM_HEADS * MEM_HD), BF16)]
    out_specs = [row(MLA_KV_RANK), row(MLA_ROPE), row(KCAT), pl.BlockSpec((MLA_HEADS, tm, KCAT), lambda i: (0, i, 0)),
                 row(POOL_WIDTH), row(LANES), row(LANES), row(MEM_HEADS * MEM_HD)]
    names = ["ckv", "kr", "kcat", "qcat", "u", "kmb", "vmb", "qmem"]
    if prompt:
        bpt = tm // MOBA_BLOCK
        nblk = n // MOBA_BLOCK
        blk3 = lambda r, c: pl.BlockSpec((bpt, r, c), lambda i: (i, 0, 0))
        out_shape += [jax.ShapeDtypeStruct((nblk, MLA_KV_RANK, MOBA_BLOCK), BF16),
                      jax.ShapeDtypeStruct((MOBA_HEADS, n, LANES), BF16),
                      jax.ShapeDtypeStruct((n, MOBA_KV_HEADS * LANES), BF16),
                      jax.ShapeDtypeStruct((nblk, LANES, MOBA_BLOCK), BF16),
                      jax.ShapeDtypeStruct((nblk, 1, MOBA_KV_HEADS * LANES), F32)]
        out_specs += [blk3(MLA_KV_RANK, MOBA_BLOCK), pl.BlockSpec((MOBA_HEADS, tm, LANES), lambda i: (0, i, 0)),
                      row(MOBA_KV_HEADS * LANES), blk3(LANES, MOBA_BLOCK), blk3(1, MOBA_KV_HEADS * LANES)]
        names += ["ckvt", "qaug", "kaug", "vt", "km"]
    else:
        out_shape.append(jax.ShapeDtypeStruct((n, MOBA_HEADS * MOBA_HD), BF16))
        out_specs.append(row(MOBA_HEADS * MOBA_HD))
        names.append("qmb")
    outs = pl.pallas_call(
        functools.partial(_proj_kernel, prompt=prompt, npos=npos),
        grid=(nt,), in_specs=in_specs, out_specs=out_specs, out_shape=out_shape,
        compiler_params=_params(), name="proj",
    )(x, lw["ln_g"], lw["wa"], lw["q_norm"], lw["kv_norm"], lw["wuq"], lw["mh"], cos, sin)
    return dict(zip(names, outs))


MLA_CHUNK_HEADS = 8


def _mla_prompt_kernel(q_ref, k_ref, vt_ref, wuv_ref, o_ref, m_ref, l_ref, acc_ref, *, tq, tk, scale):
    i = pl.program_id(1)
    n_chunks = MLA_HEADS // MLA_CHUNK_HEADS
    cw = MLA_CHUNK_HEADS * tq
    sub = tk // MOBA_BLOCK
    jd = (i * tq) // tk

    def q_chunk(c):
        return q_ref[c * MLA_CHUNK_HEADS:(c + 1) * MLA_CHUNK_HEADS].reshape(cw, KCAT)

    def k_tile(j):
        return k_ref[pl.ds(pl.multiple_of(j * tk, tk), tk), :]

    def pv(j, p):
        out = None
        for sb in range(sub):
            t = jnp.dot(vt_ref[j * sub + sb], p[sb * MOBA_BLOCK:(sb + 1) * MOBA_BLOCK], preferred_element_type=F32)
            out = t if out is None else out + t
        return out

    kd = k_tile(jd)
    kpos = jd * tk + lax.broadcasted_iota(I32, (tk, cw), 0)
    qpos = i * tq + (lax.broadcasted_iota(I32, (tk, cw), 1) & (tq - 1))
    causal = kpos <= qpos
    for c in range(n_chunks):
        s = jnp.where(causal, _bdot_nt(kd, q_chunk(c)) * scale, NEG)
        m = jnp.max(s, axis=0, keepdims=True)
        p = jnp.exp(s - m)
        m_ref[c] = m
        l_ref[c] = jnp.sum(p, axis=0, keepdims=True)
        acc_ref[c] = pv(jd, p.astype(BF16))

    def body(j, carry):
        kj = k_tile(j)
        for c in range(n_chunks):
            s = _bdot_nt(kj, q_chunk(c)) * scale
            m_old = m_ref[c]
            m_new = jnp.maximum(m_old, jnp.max(s, axis=0, keepdims=True))
            alpha = jnp.exp(m_old - m_new)
            p = jnp.exp(s - m_new)
            l_ref[c] = alpha * l_ref[c] + jnp.sum(p, axis=0, keepdims=True)
            acc_ref[c] = alpha * acc_ref[c] + pv(j, p.astype(BF16))
            m_ref[c] = m_new
        return carry

    lax.fori_loop(0, jd, body, 0)
    out = jnp.zeros((tq, MLA_HEADS * MLA_V), F32)
    for c in range(n_chunks):
        o_t = acc_ref[c] / l_ref[c]
        for hh in range(MLA_CHUNK_HEADS):
            o_lat = o_t[:, hh * tq:(hh + 1) * tq].T
            out = out + _bdot(o_lat, wuv_ref[c * MLA_CHUNK_HEADS + hh])
    o_ref[...] = out


def _mla_prompt(qcat, kcat, ckvt, wuv, batch, seq, tq, tk):
    nq = seq // tq
    nblk = seq // MOBA_BLOCK
    n_chunks = MLA_HEADS // MLA_CHUNK_HEADS
    cw = MLA_CHUNK_HEADS * tq
    return pl.pallas_call(
        functools.partial(_mla_prompt_kernel, tq=tq, tk=tk, scale=(MLA_NOPE + MLA_ROPE) ** -0.5),
        grid=(batch, nq),
        in_specs=[pl.BlockSpec((MLA_HEADS, tq, KCAT), lambda b, i: (0, b * nq + i, 0)),
                  pl.BlockSpec((seq, KCAT), lambda b, i: (b, 0)),
                  pl.BlockSpec((nblk, MLA_KV_RANK, MOBA_BLOCK), lambda b, i: (b, 0, 0)),
                  _const_spec(wuv.shape)],
        out_specs=pl.BlockSpec((tq, MLA_HEADS * MLA_V), lambda b, i: (b * nq + i, 0)),
        out_shape=jax.ShapeDtypeStruct((batch * seq, MLA_HEADS * MLA_V), F32),
        scratch_shapes=[pltpu.VMEM((n_chunks, 1, cw), F32), pltpu.VMEM((n_chunks, 1, cw), F32),
                        pltpu.VMEM((n_chunks, MLA_KV_RANK, cw), F32)],
        compiler_params=_params(2), name="mla_prompt",
    )(qcat, kcat, ckvt, wuv)


def _moba_prompt_kernel(q_ref, k_ref, vt_ref, km_ref, slope_ref, o_ref, sel_ref, m_ref, l_ref, acc_ref, *, nb, topk):
    tq = MOBA_BLOCK
    cw = MOBA_REP * tq
    ob = pl.program_id(1)
    km = km_ref[...].astype(BF16)
    rowf = lax.broadcasted_iota(I32, (nb, cw), 0).astype(F32)
    is_past = rowf < ob.astype(F32)

    def q_group(g):
        return q_ref[g * MOBA_REP:(g + 1) * MOBA_REP].reshape(cw, LANES)

    def k_block(g, jb):
        return k_ref[pl.ds(pl.multiple_of(jb * MOBA_BLOCK, MOBA_BLOCK), MOBA_BLOCK), g * LANES:(g + 1) * LANES]

    def v_block(g, jb):
        return vt_ref[jb, g * HALF:(g + 1) * HALF, :]

    causal = (lax.broadcasted_iota(I32, (MOBA_BLOCK, cw), 0)
              <= (lax.broadcasted_iota(I32, (MOBA_BLOCK, cw), 1) & (tq - 1)))
    for g in range(MOBA_KV_HEADS):
        q = q_group(g)
        gate = jnp.where(is_past, _bdot_nt(km[:, g * LANES:(g + 1) * LANES], q), NEG)
        sel, _ = _top_k_mask(gate, rowf, topk, axis=0)
        sel_ref[g] = jnp.where(is_past, sel, 0.0)
        s = jnp.where(causal, _bdot_nt(k_block(g, ob), q), NEG)
        m = jnp.max(s, axis=0, keepdims=True)
        p = jnp.exp(s - m)
        m_ref[g] = m
        l_ref[g] = jnp.sum(p, axis=0, keepdims=True)
        acc_ref[g] = jnp.dot(v_block(g, ob), p.astype(BF16), preferred_element_type=F32)

    def body(jb, carry):
        block_gap = ((ob - jb) * MOBA_BLOCK).astype(F32)
        for g in range(MOBA_KV_HEADS):
            s = _bdot_nt(k_block(g, jb), q_group(g))
            chosen = sel_ref[g, pl.ds(jb, 1), :] > 0.5
            far = slope_ref[g] * block_gap
            m_old = m_ref[g]
            m_new = jnp.maximum(m_old, jnp.where(chosen, jnp.max(s, axis=0, keepdims=True) - far, NEG))
            p = jnp.exp(s - jnp.where(chosen, m_new + far, MASK_SHIFT))
            alpha = jnp.exp(m_old - m_new)
            l_ref[g] = alpha * l_ref[g] + jnp.sum(p, axis=0, keepdims=True)
            acc_ref[g] = alpha * acc_ref[g] + jnp.dot(v_block(g, jb), p.astype(BF16), preferred_element_type=F32)
            m_ref[g] = m_new
        return carry

    lax.fori_loop(0, ob, body, 0)
    for g in range(MOBA_KV_HEADS):
        o_t = acc_ref[g] / l_ref[g]
        for pr in range(MOBA_REP // 2):
            pair = jnp.concatenate([o_t[:, 2 * pr * tq:(2 * pr + 1) * tq], o_t[:, (2 * pr + 1) * tq:(2 * pr + 2) * tq]], axis=0)
            col = (g * MOBA_REP // 2 + pr) * LANES
            o_ref[:, col:col + LANES] = pair.T


def _moba_prompt(qaug, kaug, vt, km, batch, seq):
    tq = MOBA_BLOCK
    nb = seq // MOBA_BLOCK
    gw = MOBA_KV_HEADS * LANES
    cw = MOBA_REP * tq
    slope_rows = jnp.asarray(np.repeat(np.asarray(SLOPES, np.float32), tq).reshape(MOBA_KV_HEADS, 1, cw))
    return pl.pallas_call(
        functools.partial(_moba_prompt_kernel, nb=nb, topk=min(MOBA_TOPK, nb)),
        grid=(batch, nb),
        in_specs=[pl.BlockSpec((MOBA_HEADS, tq, LANES), lambda b, i: (0, b * nb + i, 0)),
                  pl.BlockSpec((seq, gw), lambda b, i: (b, 0)),
                  pl.BlockSpec((nb, LANES, MOBA_BLOCK), lambda b, i: (b, 0, 0)),
                  pl.BlockSpec((nb, gw), lambda b, i: (b, 0)),
                  _const_spec((MOBA_KV_HEADS, 1, cw))],
        out_specs=pl.BlockSpec((tq, MOBA_HEADS * MOBA_HD), lambda b, i: (b * nb + i, 0)),
        out_shape=jax.ShapeDtypeStruct((batch * seq, MOBA_HEADS * MOBA_HD), F32),
        scratch_shapes=[pltpu.VMEM((MOBA_KV_HEADS, nb, cw), F32), pltpu.VMEM((MOBA_KV_HEADS, 1, cw), F32),
                        pltpu.VMEM((MOBA_KV_HEADS, 1, cw), F32), pltpu.VMEM((MOBA_KV_HEADS, MOBA_HD, cw), F32)],
        compiler_params=_params(2), name="moba_prompt",
    )(qaug, kaug, vt, km, slope_rows)


HALO = 16


def _pool_prompt_kernel(u_ref, halo_ref, w_ref, sc_ref, o_ref, ext_ref, *, tiles_per_seq):
    tm = u_ref.shape[0]
    i = pl.program_id(0) % tiles_per_seq
    u = u_ref[...]
    ext_ref[0:HALO, :] = jnp.where(i > 0, halo_ref[...], 0.0)
    ext_ref[HALO:HALO + tm, :] = u
    pos = i * tm + lax.broadcasted_iota(I32, (tm, 1), 0)
    parts = []
    for g, w in enumerate(POOL_WINDOWS):
        cols = slice(g * POOL_GROUP, (g + 1) * POOL_GROUP)
        tot = u[:, cols]
        for k in range(1, w):
            tot = tot + ext_ref[HALO - k:HALO - k + tm, cols]
        cnt = jnp.minimum(pos + 1, w).astype(F32)
        parts.append(_bdot(tot / cnt - u[:, cols], w_ref[g]))
    o_ref[...] = jnp.concatenate(parts, axis=-1) * sc_ref[...]


def _pool_prompt(u, w_pool, scale, seq, tm):
    n = u.shape[0]
    tps = seq // tm
    hpt = tm // HALO
    return pl.pallas_call(
        functools.partial(_pool_prompt_kernel, tiles_per_seq=tps),
        grid=(n // tm,),
        in_specs=[pl.BlockSpec((tm, POOL_WIDTH), lambda i: (i, 0)),
                  pl.BlockSpec((HALO, POOL_WIDTH), lambda i: (jnp.maximum(i * hpt - 1, 0), 0)),
                  _const_spec(w_pool.shape), _const_spec((1, POOL_WIDTH))],
        out_specs=pl.BlockSpec((tm, POOL_WIDTH), lambda i: (i, 0)),
        out_shape=jax.ShapeDtypeStruct((n, POOL_WIDTH), F32),
        scratch_shapes=[pltpu.VMEM((HALO + tm, POOL_WIDTH), F32)],
        compiler_params=_params(), name="pool_prompt",
    )(u, u, w_pool, scale)


def _pool_sample_kernel(st_ref, u_ref, w_ref, sc_ref, o_ref, *, start_pos):
    u = u_ref[...]
    parts = []
    for g, w in enumerate(POOL_WINDOWS):
        cols = slice(g * POOL_GROUP, (g + 1) * POOL_GROUP)
        tot = u[:, cols]
        for k in range(1, w):
            tot = tot + st_ref[POOL_BUF - k][:, cols]
        parts.append(_bdot(tot / float(min(start_pos + 1, w)) - u[:, cols], w_ref[g]))
    o_ref[...] = jnp.concatenate(parts, axis=-1) * sc_ref[...]


def _pool_sample(state_t, u, w_pool, scale, start_pos):
    n = u.shape[0]
    return pl.pallas_call(
        functools.partial(_pool_sample_kernel, start_pos=start_pos),
        grid=(1,),
        in_specs=[_const_spec(state_t.shape), _const_spec(u.shape), _const_spec(w_pool.shape), _const_spec((1, POOL_WIDTH))],
        out_specs=_const_spec((n, POOL_WIDTH)),
        out_shape=jax.ShapeDtypeStruct((n, POOL_WIDTH), F32),
        compiler_params=_params(), name="pool_sample",
    )(state_t, u, w_pool, scale)


def _mem_kv_kernel(mem_ref, g_ref, w_ref, k_ref, v_ref):
    kv = _bdot(_rms(mem_ref[...], g_ref[...]), w_ref[...])
    half = MEM_HEADS * MEM_HD
    k_ref[...] = kv[:, :half]
    v_ref[...] = kv[:, half:]


def _mem_kv(mem, g, w):
    n = mem.shape[0]
    half = MEM_HEADS * MEM_HD
    return pl.pallas_call(
        _mem_kv_kernel, grid=(1,),
        in_specs=[_const_spec(mem.shape), _const_spec((1, D_MODEL)), _const_spec(w.shape)],
        out_specs=[_const_spec((n, half)), _const_spec((n, half))],
        out_shape=[jax.ShapeDtypeStruct((n, half), F32), jax.ShapeDtypeStruct((n, half), F32)],
        compiler_params=_params(), name="mem_kv",
    )(mem, g, w)


def _mem_prompt_kernel(q_ref, k_ref, v_ref, o_ref):
    q = q_ref[...]
    outs = []
    for hd in range(MEM_HEADS):
        cols = slice(hd * MEM_HD, (hd + 1) * MEM_HD)
        s = _bdot_nt(q[:, cols], k_ref[:, cols]) * (MEM_HD ** -0.5)
        e = jnp.exp(s - jnp.max(s, axis=-1, keepdims=True))
        p = e / jnp.sum(e, axis=-1, keepdims=True)
        outs.append(_bdot(p, v_ref[:, cols]))
    o_ref[...] = jnp.concatenate(outs, axis=-1)


def _mem_prompt(qmem, mem_k, mem_v, seq, tm):
    n = qmem.shape[0]
    tps = seq // tm
    m_tok = mem_k.shape[0] // (n // seq)
    width = MEM_HEADS * MEM_HD
    return pl.pallas_call(
        _mem_prompt_kernel, grid=(n // tm,),
        in_specs=[pl.BlockSpec((tm, width), lambda i: (i, 0)),
                  pl.BlockSpec((m_tok, width), lambda i: (i // tps, 0)),
                  pl.BlockSpec((m_tok, width), lambda i: (i // tps, 0))],
        out_specs=pl.BlockSpec((tm, width), lambda i: (i, 0)),
        out_shape=jax.ShapeDtypeStruct((n, width), F32),
        compiler_params=_params(), name="mem_prompt",
    )(qmem, mem_k, mem_v)


def _mem_sample_kernel(q_ref, k_ref, v_ref, o_ref):
    sb = q_ref.shape[0]
    q = q_ref[...].astype(F32)
    for b in range(sb):
        s = jnp.sum(k_ref[b] * q[b][None], axis=-1, keepdims=True) * (MEM_HD ** -0.5)
        e = jnp.exp(s - jnp.max(s, axis=0, keepdims=True))
        o_ref[b] = jnp.sum(e * v_ref[b], axis=0) / jnp.sum(e, axis=0)


def _mem_sample(qmem, cache_k, cache_v, layer, sb):
    _, n, m_tok, _, _ = cache_k.shape
    blk = pl.BlockSpec((None, sb, m_tok, MEM_HEADS, MEM_HD), lambda i: (layer, i, 0, 0, 0))
    row = pl.BlockSpec((sb, MEM_HEADS, MEM_HD), lambda i: (i, 0, 0))
    return pl.pallas_call(
        _mem_sample_kernel, grid=(n // sb,),
        in_specs=[row, blk, blk], out_specs=row,
        out_shape=jax.ShapeDtypeStruct((n, MEM_HEADS, MEM_HD), F32),
        compiler_params=_params(), name="mem_sample",
    )(qmem, cache_k, cache_v)


def _merge_kernel(x_ref, g_ref, wz_ref, wgl_ref, wbr_ref, wout_ref, fn_ref, b0_ref, b1_ref, b2_ref, b3_ref, o_ref, *, final):
    x = x_ref[...]
    h = _rms(x, g_ref[...]).astype(BF16)
    acc = jnp.zeros(x.shape, F32)
    for b, br in enumerate((b0_ref, b1_ref, b2_ref, b3_ref)):
        z = jnp.dot(h, wz_ref[:, b * BRANCH_WIDTH:(b + 1) * BRANCH_WIDTH], preferred_element_type=F32)
        gl = jnp.dot(h, wgl_ref[:, b * D_MODEL:(b + 1) * D_MODEL], preferred_element_type=F32)
        o = br[...] * (z * _sigmoid(z))
        acc = acc + _sigmoid(gl) * _bdot(o, wbr_ref[b])
    y = x + _bdot(acc, wout_ref[...])
    if final:
        y = _rms(y, fn_ref[...])
    o_ref[...] = y


def _merge(x, lw, final_norm, branches, tm, final):
    n = x.shape[0]
    row = lambda w: pl.BlockSpec((tm, w), lambda i: (i, 0))
    return pl.pallas_call(
        functools.partial(_merge_kernel, final=final),
        grid=(n // tm,),
        in_specs=[row(D_MODEL), _const_spec((1, D_MODEL)), _const_spec(lw["wz"].shape), _const_spec(lw["wgl"].shape),
                  _const_spec(lw["wbr"].shape), _const_spec(lw["wout"].shape), _const_spec((1, D_MODEL))]
                 + [row(BRANCH_WIDTH)] * N_BRANCH,
        out_specs=row(D_MODEL),
        out_shape=jax.ShapeDtypeStruct((n, D_MODEL), F32),
        compiler_params=_params(), name="merge",
    )(x, lw["ln_g"], lw["wz"], lw["wgl"], lw["wbr"], lw["wout"], final_norm, *branches)


N_SLOTS = 3
PREFETCH = N_SLOTS - 1
MAX_CHUNK_PAGES = 16


def _chunking(n_pages):
    ch = max(c for c in range(1, MAX_CHUNK_PAGES + 1) if n_pages % c == 0 and n_pages // c >= PREFETCH)
    return ch, n_pages // ch


def _mla_sample_kernel(pt_ref, q_ref, kn_ref, ckv_hbm, krt_hbm, o_ref, kbuf, rbuf, sem, *, layer, ch, nch, scale):
    b = pl.program_id(0)
    total = pl.num_programs(0) * nch

    def copies(t):
        bb = t // nch
        c = t % nch
        slot = t % N_SLOTS
        out = []
        for p in range(ch):
            page = pt_ref[bb, c * ch + p]
            out.append(pltpu.make_async_copy(ckv_hbm.at[layer, page], kbuf.at[slot, pl.ds(p * PAGE_SIZE, PAGE_SIZE)],
                                             sem.at[0, slot]))
            out.append(pltpu.make_async_copy(krt_hbm.at[layer, page], rbuf.at[slot, p], sem.at[1, slot]))
        return out

    @pl.when(b == 0)
    def _():
        for t in range(PREFETCH):
            for cp in copies(t):
                cp.start()

    q = q_ref[0]
    q_lat = q[:, :MLA_KV_RANK]
    q_rope = q[:, MLA_KV_RANK:MLA_KV_RANK + MLA_ROPE]
    kn = kn_ref[0]
    m = jnp.sum(q.astype(F32) * kn.astype(F32), axis=-1, keepdims=True) * scale
    l = jnp.ones_like(m)
    acc = jnp.broadcast_to(kn[:, :MLA_KV_RANK].astype(F32), (MLA_HEADS, MLA_KV_RANK))
    for c in range(nch):
        t = b * nch + c
        slot = t % N_SLOTS

        @pl.when(t + PREFETCH < total)
        def _():
            for cp in copies(t + PREFETCH):
                cp.start()

        for cp in copies(t):
            cp.wait()
        kb = kbuf[slot].astype(BF16)
        s_rope = jnp.concatenate([_bdot(q_rope, rbuf[slot, p]) for p in range(ch)], axis=-1)
        s = (_bdot_nt(q_lat, kb) + s_rope) * scale
        m_new = jnp.maximum(m, jnp.max(s, axis=-1, keepdims=True))
        alpha = jnp.exp(m - m_new)
        p = jnp.exp(s - m_new)
        l = alpha * l + jnp.sum(p, axis=-1, keepdims=True)
        acc = alpha * acc + jnp.dot(p.astype(BF16), kb, preferred_element_type=F32)
        m = m_new
    o_ref[0] = acc / l


def _mla_sample(page_table, q, knew, cache_ckv, cache_krt, layer):
    ndb, n_pages = page_table.shape
    ch, nch = _chunking(n_pages)
    rows = ch * PAGE_SIZE
    return pl.pallas_call(
        functools.partial(_mla_sample_kernel, layer=layer, ch=ch, nch=nch, scale=(MLA_NOPE + MLA_ROPE) ** -0.5),
        grid_spec=pltpu.PrefetchScalarGridSpec(
            num_scalar_prefetch=1, grid=(ndb,),
            in_specs=[pl.BlockSpec((1, MLA_HEADS, KCAT), lambda b, pt: (b, 0, 0)),
                      pl.BlockSpec((1, 1, KCAT), lambda b, pt: (b, 0, 0)),
                      pl.BlockSpec(memory_space=pl.ANY), pl.BlockSpec(memory_space=pl.ANY)],
            out_specs=pl.BlockSpec((1, MLA_HEADS, MLA_KV_RANK), lambda b, pt: (b, 0, 0)),
            scratch_shapes=[pltpu.VMEM((N_SLOTS, rows, MLA_KV_RANK), F32), pltpu.VMEM((N_SLOTS, ch, MLA_ROPE, PAGE_SIZE), F32),
                            pltpu.SemaphoreType.DMA((2, N_SLOTS))]),
        out_shape=jax.ShapeDtypeStruct((ndb, MLA_HEADS, MLA_KV_RANK), F32),
        compiler_params=_params(), name="mla_sample",
    )(page_table, q, knew, cache_ckv, cache_krt)


def _uv_kernel(o_ref, wuv_ref, y_ref):
    out = jnp.zeros(y_ref.shape, F32)
    for hd in range(MLA_HEADS):
        out = out + _bdot(o_ref[hd], wuv_ref[hd])
    y_ref[...] = out


def _uv_project(o_lat, wuv):
    n = o_lat.shape[1]
    return pl.pallas_call(
        _uv_kernel, grid=(1,),
        in_specs=[_const_spec(o_lat.shape), _const_spec(wuv.shape)],
        out_specs=_const_spec((n, MLA_HEADS * MLA_V)),
        out_shape=jax.ShapeDtypeStruct((n, MLA_HEADS * MLA_V), F32),
        compiler_params=_params(), name="mla_uv",
    )(o_lat, wuv)


def _moba_gate_kernel(pt_ref, q_ref, k_hbm, idx_ref, kbuf, km_ref, sem, *, layer, ch, nch, own, topk):
    b = pl.program_id(0)
    total = pl.num_programs(0) * nch

    def copies(t):
        bb = t // nch
        c = t % nch
        slot = t % N_SLOTS
        return [pltpu.make_async_copy(k_hbm.at[layer, pt_ref[bb, c * ch + p]], kbuf.at[slot, p], sem.at[slot])
                for p in range(ch)]

    @pl.when(b == 0)
    def _():
        for t in range(PREFETCH):
            for cp in copies(t):
                cp.start()

    bpc = ch // MOBA_PPB
    for c in range(nch):
        t = b * nch + c
        slot = t % N_SLOTS

        @pl.when(t + PREFETCH < total)
        def _():
            for cp in copies(t + PREFETCH):
                cp.start()

        for cp in copies(t):
            cp.wait()
        for bi in range(bpc):
            n = c * bpc + bi
            for g in range(MOBA_KV_HEADS):
                tot = kbuf[slot, bi * MOBA_PPB, g]
                for j in range(1, MOBA_PPB):
                    tot = tot + kbuf[slot, bi * MOBA_PPB + j, g]
                km_ref[g, :, n:n + 1] = jnp.sum(tot, axis=-1, keepdims=True) * (1.0 / MOBA_BLOCK)
    q = q_ref[0]
    nblk = km_ref.shape[2]
    g0 = _bdot(q, km_ref[0])
    g1 = _bdot(q, km_ref[1])
    head = lax.broadcasted_iota(I32, (MOBA_HEADS, nblk), 0)
    colf = lax.broadcasted_iota(I32, (MOBA_HEADS, nblk), 1).astype(F32)
    gate = jnp.where(head < MOBA_REP, g0, g1)
    gate = jnp.where(colf < float(own), gate, NEG)
    _, picked = _top_k_mask(gate, colf, topk)
    lane = lax.broadcasted_iota(I32, (MOBA_HEADS, LANES), 1)
    out = jnp.zeros((MOBA_HEADS, LANES), F32)
    for t, idx in enumerate(picked):
        out = jnp.where(lane == t, idx, out)
    idx_ref[0] = out.astype(I32)


def _moba_gate(page_table, q, cache_kt, layer, n_full, own, topk):
    ndb = page_table.shape[0]
    ch, nch = _chunking(n_full * MOBA_PPB)
    return pl.pallas_call(
        functools.partial(_moba_gate_kernel, layer=layer, ch=ch, nch=nch, own=own, topk=topk),
        grid_spec=pltpu.PrefetchScalarGridSpec(
            num_scalar_prefetch=1, grid=(ndb,),
            in_specs=[pl.BlockSpec((1, MOBA_HEADS, MOBA_HD), lambda b, pt: (b, 0, 0)),
                      pl.BlockSpec(memory_space=pl.ANY)],
            out_specs=pl.BlockSpec((1, MOBA_HEADS, LANES), lambda b, pt: (b, 0, 0)),
            scratch_shapes=[pltpu.VMEM((N_SLOTS, ch, MOBA_KV_HEADS, MOBA_HD, PAGE_SIZE), F32),
                            pltpu.VMEM((MOBA_KV_HEADS, MOBA_HD, n_full), F32),
                            pltpu.SemaphoreType.DMA((N_SLOTS,))]),
        out_shape=jax.ShapeDtypeStruct((ndb, MOBA_HEADS, LANES), I32),
        compiler_params=_params(), name="moba_gate",
    )(page_table, q, cache_kt)


def _moba_sample_kernel(pt_ref, idx_ref, qt_ref, knt_ref, vnt_ref, k_hbm, v_hbm, ot_ref, kbuf, vbuf, sem,
                        *, layer, topk, own, q_pos):
    b = pl.program_id(0)
    nb = pl.num_programs(0)

    def copies(bb, slot):
        out = []
        for hd in range(MOBA_HEADS):
            g = hd // MOBA_REP
            for t in range(topk):
                blk = idx_ref[bb, hd * topk + t]
                for j in range(MOBA_PPB):
                    page = pt_ref[bb, blk * MOBA_PPB + j]
                    dst = (hd * topk + t) * MOBA_PPB + j
                    out.append(pltpu.make_async_copy(k_hbm.at[layer, page, g], kbuf.at[slot, dst], sem.at[0, slot]))
                    out.append(pltpu.make_async_copy(v_hbm.at[layer, page, g], vbuf.at[slot, dst], sem.at[1, slot]))
        return out

    slot = b % 2

    @pl.when(b == 0)
    def _():
        for cp in copies(0, 0):
            cp.start()

    @pl.when(b + 1 < nb)
    def _():
        for cp in copies(b + 1, 1 - slot):
            cp.start()

    for cp in copies(b, slot):
        cp.wait()

    qt = qt_ref[0]
    knt = knt_ref[0]
    vnt = vnt_ref[0]
    lane = lax.broadcasted_iota(I32, (1, PAGE_SIZE), 1)
    for hd in range(MOBA_HEADS):
        g = hd // MOBA_REP
        qcol = qt[:, hd:hd + 1]
        pieces = []
        for t in range(topk):
            blk = idx_ref[b, hd * topk + t]
            for j in range(MOBA_PPB):
                kt = kbuf[slot, (hd * topk + t) * MOBA_PPB + j]
                dist = (q_pos - (blk * MOBA_BLOCK + j * PAGE_SIZE + lane)).astype(F32)
                s = jnp.sum(kt * qcol, axis=0, keepdims=True) - SLOPES[hd] * dist
                pieces.append(jnp.where(blk < own, s, NEG))
        s_self = jnp.sum(qcol * knt[:, g:g + 1], axis=0, keepdims=True)
        smax = pieces[0]
        for s in pieces[1:]:
            smax = jnp.maximum(smax, s)
        m = jnp.maximum(jnp.max(smax, axis=-1, keepdims=True), s_self)
        p_self = jnp.exp(s_self - m)
        den = p_self
        wsum = jnp.zeros((MOBA_HD, PAGE_SIZE), F32)
        for pi, s in enumerate(pieces):
            p = jnp.exp(s - m)
            den = den + jnp.sum(p, axis=-1, keepdims=True)
            wsum = wsum + vbuf[slot, hd * topk * MOBA_PPB + pi] * p
        o = (jnp.sum(wsum, axis=-1, keepdims=True) + p_self * vnt[:, g:g + 1]) / den
        ot_ref[0, :, hd:hd + 1] = o


def _moba_sample(page_table, idx, qt, knt, vnt, cache_kt, cache_vt, layer, topk, own, q_pos):
    ndb = page_table.shape[0]
    n_buf = MOBA_HEADS * topk * MOBA_PPB
    col = lambda w: pl.BlockSpec((1, MOBA_HD, w), lambda b, pt, ix: (b, 0, 0))
    return pl.pallas_call(
        functools.partial(_moba_sample_kernel, layer=layer, topk=topk, own=own, q_pos=q_pos),
        grid_spec=pltpu.PrefetchScalarGridSpec(
            num_scalar_prefetch=2, grid=(ndb,),
            in_specs=[col(MOBA_HEADS), col(MOBA_KV_HEADS), col(MOBA_KV_HEADS),
                      pl.BlockSpec(memory_space=pl.ANY), pl.BlockSpec(memory_space=pl.ANY)],
            out_specs=col(MOBA_HEADS),
            scratch_shapes=[pltpu.VMEM((2, n_buf, MOBA_HD, PAGE_SIZE), F32),
                            pltpu.VMEM((2, n_buf, MOBA_HD, PAGE_SIZE), F32),
                            pltpu.SemaphoreType.DMA((2, 2))]),
        out_shape=jax.ShapeDtypeStruct((ndb, MOBA_HD, MOBA_HEADS), F32),
        compiler_params=_params(), name="moba_sample",
    )(page_table, idx, qt, knt, vnt, cache_kt, cache_vt)


def _rope_tables(pos):
    half = MLA_ROPE // 2
    inv = ROPE_BASE ** (-jnp.arange(half, dtype=F32) / half)
    ang = pos.astype(F32)[:, None] * inv[None, :]
    cos, sin = jnp.cos(ang), jnp.sin(ang)
    reps = 2 * LANES // MLA_ROPE
    return jnp.tile(jnp.concatenate([cos, cos], axis=-1), (1, reps)), jnp.tile(jnp.concatenate([-sin, sin], axis=-1), (1, reps))


def _pad_lanes(w):
    return jnp.pad(w, ((0, 0), (0, LANES - w.shape[1])))


def _layer_weights(l, ln_g, w_in, mla_q_norm, mla_w_uq, mla_kv_norm, mla_w_uk, mla_w_uv, pool_w, pool_scale,
                   mem_norm, w_mem_kv, w_branch, w_out):
    half = MLA_ROPE // 2
    swap = jnp.concatenate([jnp.arange(half, MLA_ROPE), jnp.arange(half)])
    offs = [0]
    for s in IN_SIZES:
        offs.append(offs[-1] + s)
    c_q, c_kv, k_rope, u, q_mb, k_mb, v_mb, q_mem, z, gl = [w_in[l][:, offs[i]:offs[i + 1]] for i in range(len(IN_SIZES))]
    wa = jnp.concatenate([c_q, c_kv, u, q_mb, k_mb, v_mb, q_mem, _pad_lanes(k_rope), _pad_lanes(k_rope[:, swap])], axis=1)
    uq = mla_w_uq[l]
    qr = uq.shape[0]
    wuq = jnp.concatenate([uq[:, :, :MLA_NOPE].reshape(qr, -1), uq[:, :, MLA_NOPE:].reshape(qr, -1),
                           uq[:, :, MLA_NOPE:][:, :, swap].reshape(qr, -1)], axis=1)
    mh = jnp.zeros((MLA_HEADS, 2 * LANES, KCAT), F32)
    eye = jnp.eye(MLA_ROPE, dtype=F32)
    for hd in range(MLA_HEADS):
        r0 = (hd * MLA_NOPE) % LANES
        mh = mh.at[hd, r0:r0 + MLA_NOPE, :MLA_KV_RANK].set(mla_w_uk[l][:, hd, :].T)
        r1 = LANES + (hd * MLA_ROPE) % LANES
        mh = mh.at[hd, r1:r1 + MLA_ROPE, MLA_KV_RANK:MLA_KV_RANK + MLA_ROPE].set(eye)
    wuv = jnp.zeros((MLA_HEADS, MLA_KV_RANK, MLA_HEADS * MLA_V), F32)
    for hd in range(MLA_HEADS):
        wuv = wuv.at[hd, :, hd * MLA_V:(hd + 1) * MLA_V].set(mla_w_uv[l][:, hd, :])
    return dict(
        ln_g=ln_g[l][None, :], wa=wa.astype(BF16), q_norm=mla_q_norm[l][None, :], kv_norm=mla_kv_norm[l][None, :],
        wuq=wuq.astype(BF16), mh=mh.astype(BF16), wuv=wuv.astype(BF16), wz=z.astype(BF16), wgl=gl.astype(BF16),
        wbr=w_branch[l].astype(BF16), wout=w_out[l].astype(BF16), pool_w=pool_w[l].astype(BF16),
        pool_scale=pool_scale[l][None, :], mem_norm=mem_norm[l][None, :], w_mem_kv=w_mem_kv[l].astype(BF16))


def kernel(x_prompt, x_sample, cache_mla_ckv, cache_mla_krope, cache_moba_k, cache_moba_v, state_pool, cache_mem_k, cache_mem_v, page_table, mem_prompt, ln_g, w_in, mla_q_norm, mla_w_uq, mla_kv_norm, mla_w_uk, mla_w_uv, pool_w, pool_scale, mem_norm, w_mem_kv, w_branch, w_out, final_norm):
    batch, seq, _ = x_prompt.shape
    ndb, t_new, _ = x_sample.shape
    depth = ln_g.shape[0]
    n_pages = page_table.shape[1]
    past = n_pages * PAGE_SIZE
    m_tok = mem_prompt.shape[1]
    assert t_new == 1, "one new token per sample"
    assert past % MOBA_BLOCK == 0, "no partially filled cached MoBA block"
    assert seq % MOBA_BLOCK == 0 and ndb % 8 == 0 and MOBA_PPB * PAGE_SIZE == MOBA_BLOCK
    tm = MOBA_BLOCK
    tq = min(512, seq)
    tk = min(512, seq)
    n_full = past // MOBA_BLOCK
    own_s = past // MOBA_BLOCK
    topk_s = min(MOBA_TOPK, n_full)
    cos_p, sin_p = _rope_tables(jnp.arange(seq))
    cos_s, sin_s = _rope_tables(jnp.full((ndb,), past))
    fnorm = final_norm[None, :]
    cache_krt = cache_mla_krope.transpose(0, 1, 3, 2)
    cache_kt = cache_moba_k.transpose(0, 1, 3, 4, 2)
    cache_vt = cache_moba_v.transpose(0, 1, 3, 4, 2)

    xp = x_prompt.reshape(batch * seq, D_MODEL)
    xs = x_sample.reshape(ndb, D_MODEL)
    mem_flat = mem_prompt.reshape(batch * m_tok, D_MODEL)
    outs = {k: [] for k in ("ckv_p", "kr_p", "mbk_p", "mbv_p", "pool_p", "memk_p", "memv_p",
                            "ckv_s", "kr_s", "mbk_s", "mbv_s", "pool_s")}
    for l in range(depth):
        lw = _layer_weights(l, ln_g, w_in, mla_q_norm, mla_w_uq, mla_kv_norm, mla_w_uk, mla_w_uv, pool_w, pool_scale,
                            mem_norm, w_mem_kv, w_branch, w_out)
        last = l == depth - 1
        pj = _project(xp, lw, cos_p, sin_p, tm, True)
        o_mla = _mla_prompt(pj["qcat"], pj["kcat"], pj["ckvt"], lw["wuv"], batch, seq, tq, tk)
        o_pool = _pool_prompt(pj["u"], lw["pool_w"], lw["pool_scale"], seq, tm)
        km = pj["km"].reshape(batch * seq // MOBA_BLOCK, MOBA_KV_HEADS * LANES)
        o_moba = _moba_prompt(pj["qaug"], pj["kaug"], pj["vt"], km, batch, seq)
        mem_k, mem_v = _mem_kv(mem_flat, lw["mem_norm"], lw["w_mem_kv"])
        o_mem = _mem_prompt(pj["qmem"], mem_k, mem_v, seq, tm)
        xp = _merge(xp, lw, fnorm, (o_mla, o_pool, o_moba, o_mem), tm, last)
        outs["ckv_p"].append(pj["ckv"].reshape(batch, seq, MLA_KV_RANK))
        outs["kr_p"].append(pj["kr"].reshape(batch, seq, MLA_ROPE))
        outs["mbk_p"].append(pj["kmb"].reshape(batch, seq, MOBA_KV_HEADS, MOBA_HD))
        outs["mbv_p"].append(pj["vmb"].reshape(batch, seq, MOBA_KV_HEADS, MOBA_HD))
        outs["pool_p"].append(pj["u"].reshape(batch, seq, POOL_WIDTH)[:, seq - POOL_BUF:])
        outs["memk_p"].append(mem_k.reshape(batch, m_tok, MEM_HEADS, MEM_HD))
        outs["memv_p"].append(mem_v.reshape(batch, m_tok, MEM_HEADS, MEM_HD))
        sj = _project(xs, lw, cos_s, sin_s, ndb, False)
        o_lat = _mla_sample(page_table, sj["qcat"].transpose(1, 0, 2), sj["kcat"][:, None, :], cache_mla_ckv,
                            cache_krt, l)
        o_mla = _uv_project(o_lat.transpose(1, 0, 2), lw["wuv"])
        u_s = sj["u"]
        o_pool = _pool_sample(state_pool[l].transpose(1, 0, 2), u_s, lw["pool_w"], lw["pool_scale"], past)
        q_heads = sj["qmb"].reshape(ndb, MOBA_HEADS, MOBA_HD)
        knew = sj["kmb"].reshape(ndb, MOBA_KV_HEADS, MOBA_HD)
        vnew = sj["vmb"].reshape(ndb, MOBA_KV_HEADS, MOBA_HD)
        idx = _moba_gate(page_table, q_heads, cache_kt, l, n_full, own_s, topk_s)
        idx = idx[:, :, :topk_s].reshape(ndb, MOBA_HEADS * topk_s)
        o_moba_t = _moba_sample(page_table, idx, q_heads.astype(F32).transpose(0, 2, 1), knew.transpose(0, 2, 1),
                                vnew.transpose(0, 2, 1), cache_kt, cache_vt, l, topk_s, own_s, past)
        o_moba = o_moba_t.transpose(0, 2, 1).reshape(ndb, MOBA_HEADS * MOBA_HD)
        o_mem = _mem_sample(sj["qmem"].reshape(ndb, MEM_HEADS, MEM_HD), cache_mem_k, cache_mem_v, l, 8)
        o_mem = o_mem.reshape(ndb, MEM_HEADS * MEM_HD)
        xs = _merge(xs, lw, fnorm, (o_mla, o_pool, o_moba, o_mem), ndb, last)
        outs["ckv_s"].append(sj["ckv"][:, None, :])
        outs["kr_s"].append(sj["kr"][:, None, :])
        outs["mbk_s"].append(knew[:, None])
        outs["mbv_s"].append(vnew[:, None])
        outs["pool_s"].append(jnp.concatenate([state_pool[l][:, 1:], u_s[:, None, :]], axis=1))
    st = lambda k: jnp.stack(outs[k])
    return (xp.reshape(batch, seq, D_MODEL), xs.reshape(ndb, t_new, D_MODEL), st("ckv_p"), st("kr_p"), st("mbk_p"),
            st("mbv_p"), st("pool_p"), st("memk_p"), st("memv_p"), st("ckv_s"), st("kr_s"), st("mbk_s"), st("mbv_s"),
            st("pool_s"))
```

```python
import functools

import jax
import jax.numpy as jnp
import numpy as np
from jax import lax
from jax.experimental import pallas as pl
from jax.experimental.pallas import tpu as pltpu

F32 = jnp.float32
BF16 = jnp.bfloat16
I32 = jnp.int32

D_MODEL = 1024
PAGE_SIZE = 128
N_BRANCH = 4
BRANCH_WIDTH = 512
MLA_HEADS = 8
MLA_NOPE = 64
MLA_ROPE = 32
MLA_V = 64
MLA_Q_RANK = 384
MLA_KV_RANK = 256
ROPE_BASE = 10000.0
POOL_WINDOWS = (2, 4, 8, 16)
POOL_GROUP = 128
POOL_WIDTH = POOL_GROUP * len(POOL_WINDOWS)
POOL_BUF = max(POOL_WINDOWS) - 1
MOBA_HEADS = 8
MOBA_KV_HEADS = 2
MOBA_REP = MOBA_HEADS // MOBA_KV_HEADS
MOBA_HD = 64
MOBA_BLOCK = 256
MOBA_TOPK = 3
MOBA_PPB = 2
MEM_HEADS = 4
MEM_HD = 128
NORM_EPS = 1e-6
NEG = -1e30
KNOCKOUT = -3.0e38
MASK_SHIFT = 1e30
IN_SIZES = (MLA_Q_RANK, MLA_KV_RANK, MLA_ROPE, POOL_WIDTH, MOBA_HEADS * MOBA_HD, MOBA_KV_HEADS * MOBA_HD,
            MOBA_KV_HEADS * MOBA_HD, MEM_HEADS * MEM_HD, N_BRANCH * BRANCH_WIDTH, N_BRANCH * D_MODEL)
SLOPES = tuple(2.0 ** (-8.0 * (h + 1) / MOBA_HEADS) for h in range(MOBA_HEADS))

LANES = 128
KCAT = MLA_KV_RANK + LANES
HALF = LANES // 2
VMEM_LIMIT = 56 * 1024 * 1024

A_CQ = 0
A_CKV = A_CQ + MLA_Q_RANK
A_U = A_CKV + MLA_KV_RANK
A_QMB = A_U + POOL_WIDTH
A_KMB = A_QMB + MOBA_HEADS * MOBA_HD
A_VMB = A_KMB + LANES
A_QMEM = A_VMB + LANES
A_KRA = A_QMEM + MEM_HEADS * MEM_HD
A_KRB = A_KRA + LANES
A_END = A_KRB + LANES


def _rms(x, g):
    return x * lax.rsqrt(jnp.mean(x * x, axis=-1, keepdims=True) + NORM_EPS) * g


def _bdot(a, b):
    return jnp.dot(a.astype(BF16), b.astype(BF16), preferred_element_type=F32)


def _bdot_nt(a, b):
    return lax.dot_general(a.astype(BF16), b.astype(BF16), (((1,), (1,)), ((), ())), preferred_element_type=F32)


def _sigmoid(x):
    return 1.0 / (1.0 + jnp.exp(-x))


def _const_spec(shape):
    nd = len(shape)
    return pl.BlockSpec(shape, lambda *_: (0,) * nd, pipeline_mode=pl.Buffered(1))


def _params(n_axes=1):
    return pltpu.CompilerParams(dimension_semantics=("arbitrary",) * n_axes, vmem_limit_bytes=VMEM_LIMIT)


def _top_k_mask(gate, colf, k, axis=-1):
    sel = jnp.zeros_like(gate)
    picked = []
    big = jnp.float32(gate.shape[axis])
    for _ in range(k):
        mx = jnp.max(gate, axis=axis, keepdims=True)
        idx = jnp.min(jnp.where(gate == mx, colf, big), axis=axis, keepdims=True)
        hit = colf == idx
        sel = jnp.where(hit, 1.0, sel)
        gate = jnp.where(hit, KNOCKOUT, gate)
        picked.append(idx)
    return sel, picked


N_COMMON = 8


def _proj_kernel(*refs, prompt, npos):
    x_ref, g_ref, wa_ref, qn_ref, kvn_ref, wuq_ref, mh_ref, cos_ref, sin_ref = refs[:9]
    ckv_ref, kr_ref, kcat_ref, qcat_ref, u_ref, kmb_ref, vmb_ref, qmem_ref = refs[9:9 + N_COMMON]
    extra = refs[9 + N_COMMON:]
    tm = x_ref.shape[0]
    h = _rms(x_ref[...], g_ref[...])
    y = _bdot(h, wa_ref[...])
    cos = cos_ref[...]
    sin = sin_ref[...]
    ckv = _rms(y[:, A_CKV:A_U], kvn_ref[...])
    ckv_ref[...] = ckv
    kr = y[:, A_KRA:A_KRB] * cos[:, :LANES] + y[:, A_KRB:A_END] * sin[:, :LANES]
    kr_ref[...] = kr[:, :MLA_ROPE]
    kcat_ref[...] = jnp.concatenate([ckv.astype(BF16), kr.astype(BF16)], axis=-1)
    q1 = _bdot(_rms(y[:, A_CQ:A_CKV], qn_ref[...]), wuq_ref[...])
    n_nope = MLA_HEADS * MLA_NOPE
    n_rope = MLA_HEADS * MLA_ROPE
    roped = q1[:, n_nope:n_nope + n_rope] * cos + q1[:, n_nope + n_rope:] * sin
    for hd in range(MLA_HEADS):
        pair = (hd * MLA_NOPE) // LANES
        quad = (hd * MLA_ROPE) // LANES
        lhs = jnp.concatenate([q1[:, pair * LANES:(pair + 1) * LANES], roped[:, quad * LANES:(quad + 1) * LANES]], axis=-1)
        qcat_ref[hd] = _bdot(lhs, mh_ref[hd]).astype(BF16)
    u_ref[...] = y[:, A_U:A_QMB]
    qs = y[:, A_QMB:A_KMB] * (MOBA_HD ** -0.5)
    kmb = y[:, A_KMB:A_VMB]
    vmb = y[:, A_VMB:A_QMEM]
    kmb_ref[...] = kmb
    vmb_ref[...] = vmb
    qmem_ref[...] = y[:, A_QMEM:A_KRA].astype(BF16)
    if not prompt:
        extra[0][...] = qs.astype(BF16)
        return
    ckvt_ref, qaug_ref, kaug_ref, vt_ref, km_ref = extra
    pos0 = (pl.program_id(0) % npos) * tm
    within = ((pos0 + lax.broadcasted_iota(I32, (tm, 1), 0)) & (MOBA_BLOCK - 1)).astype(F32)
    lane = lax.broadcasted_iota(I32, (tm, LANES), 1)
    low = lane < HALF
    for hd in range(MOBA_HEADS):
        blk = qs[:, (hd // 2) * LANES:(hd // 2 + 1) * LANES]
        if hd % 2:
            blk = pltpu.roll(blk, HALF, 1)
        a = jnp.where(low, blk, 0.0)
        a = jnp.where(lane == HALF, SLOPES[hd], a)
        a = jnp.where(lane == HALF + 1, -SLOPES[hd] * within, a)
        qaug_ref[hd] = a.astype(BF16)
    kparts = []
    for g in range(MOBA_KV_HEADS):
        blk = kmb if g == 0 else pltpu.roll(kmb, HALF, 1)
        a = jnp.where(low, blk, 0.0)
        a = jnp.where(lane == HALF, within, a)
        a = jnp.where(lane == HALF + 1, 1.0, a)
        kparts.append(a)
    kaug_ref[...] = jnp.concatenate(kparts, axis=-1).astype(BF16)
    low1 = lax.broadcasted_iota(I32, (1, LANES), 1) < HALF
    for j in range(tm // MOBA_BLOCK):
        rows = slice(j * MOBA_BLOCK, (j + 1) * MOBA_BLOCK)
        vt_ref[j] = vmb[rows].T.astype(BF16)
        ckvt_ref[j] = ckv[rows].T.astype(BF16)
        mean = jnp.sum(kmb[rows], axis=0, keepdims=True) * (1.0 / MOBA_BLOCK)
        km_ref[j] = jnp.concatenate([jnp.where(low1, mean, 0.0), jnp.where(low1, pltpu.roll(mean, HALF, 1), 0.0)], axis=-1)


def _project(x, lw, cos, sin, tm, prompt):
    n = x.shape[0]
    nt = n // tm
    npos = cos.shape[0] // tm
    row = lambda w: pl.BlockSpec((tm, w), lambda i: (i, 0))
    in_specs = [row(D_MODEL), _const_spec((1, D_MODEL)), _const_spec((D_MODEL, A_END)), _const_spec((1, MLA_Q_RANK)),
                _const_spec((1, MLA_KV_RANK)), _const_spec(lw["wuq"].shape), _const_spec(lw["mh"].shape),
                pl.BlockSpec((tm, 2 * LANES), lambda i: (i % npos, 0)), pl.BlockSpec((tm, 2 * LANES), lambda i: (i % npos, 0))]
    out_shape = [jax.ShapeDtypeStruct((n, MLA_KV_RANK), F32), jax.ShapeDtypeStruct((n, MLA_ROPE), F32),
                 jax.ShapeDtypeStruct((n, KCAT), BF16), jax.ShapeDtypeStruct((MLA_HEADS, n, KCAT), BF16),
                 jax.ShapeDtypeStruct((n, POOL_WIDTH), F32), jax.ShapeDtypeStruct((n, LANES), F32),
                 jax.ShapeDtypeStruct((n, LANES), F32), jax.ShapeDtypeStruct((n, MEM_HEADS * MEM_HD), BF16)]
    out_specs = [row(MLA_KV_RANK), row(MLA_ROPE), row(KCAT), pl.BlockSpec((MLA_HEADS, tm, KCAT), lambda i: (0, i, 0)),
                 row(POOL_WIDTH), row(LANES), row(LANES), row(MEM_HEADS * MEM_HD)]
    names = ["ckv", "kr", "kcat", "qcat", "u", "kmb", "vmb", "qmem"]
    if prompt:
        bpt = tm // MOBA_BLOCK
        nblk = n // MOBA_BLOCK
        blk3 = lambda r, c: pl.BlockSpec((bpt, r, c), lambda i: (i, 0, 0))
        out_shape += [jax.ShapeDtypeStruct((nblk, MLA_KV_RANK, MOBA_BLOCK), BF16),
                      jax.ShapeDtypeStruct((MOBA_HEADS, n, LANES), BF16),
                      jax.ShapeDtypeStruct((n, MOBA_KV_HEADS * LANES), BF16),
                      jax.ShapeDtypeStruct((nblk, LANES, MOBA_BLOCK), BF16),
                      jax.ShapeDtypeStruct((nblk, 1, MOBA_KV_HEADS * LANES), F32)]
        out_specs += [blk3(MLA_KV_RANK, MOBA_BLOCK), pl.BlockSpec((MOBA_HEADS, tm, LANES), lambda i: (0, i, 0)),
                      row(MOBA_KV_HEADS * LANES), blk3(LANES, MOBA_BLOCK), blk3(1, MOBA_KV_HEADS * LANES)]
        names += ["ckvt", "qaug", "kaug", "vt", "km"]
    else:
        out_shape.append(jax.ShapeDtypeStruct((n, MOBA_HEADS * MOBA_HD), BF16))
        out_specs.append(row(MOBA_HEADS * MOBA_HD))
        names.append("qmb")
    outs = pl.pallas_call(
        functools.partial(_proj_kernel, prompt=prompt, npos=npos),
        grid=(nt,), in_specs=in_specs, out_specs=out_specs, out_shape=out_shape,
        compiler_params=_params(), name="proj",
    )(x, lw["ln_g"], lw["wa"], lw["q_norm"], lw["kv_norm"], lw["wuq"], lw["mh"], cos, sin)
    return dict(zip(names, outs))


MLA_CHUNK_HEADS = 8


def _mla_prompt_kernel(q_ref, k_ref, vt_ref, wuv_ref, o_ref, m_ref, l_ref, acc_ref, *, tq, tk, scale):
    i = pl.program_id(1)
    n_chunks = MLA_HEADS // MLA_CHUNK_HEADS
    cw = MLA_CHUNK_HEADS * tq
    sub = tk // MOBA_BLOCK
    jd = (i * tq) // tk

    def q_chunk(c):
        return q_ref[c * MLA_CHUNK_HEADS:(c + 1) * MLA_CHUNK_HEADS].reshape(cw, KCAT)

    def k_tile(j):
        return k_ref[pl.ds(pl.multiple_of(j * tk, tk), tk), :]

    def pv(j, p):
        out = None
        for sb in range(sub):
            t = jnp.dot(vt_ref[j * sub + sb], p[sb * MOBA_BLOCK:(sb + 1) * MOBA_BLOCK], preferred_element_type=F32)
            out = t if out is None else out + t
        return out

    kd = k_tile(jd)
    kpos = jd * tk + lax.broadcasted_iota(I32, (tk, cw), 0)
    qpos = i * tq + (lax.broadcasted_iota(I32, (tk, cw), 1) & (tq - 1))
    causal = kpos <= qpos
    for c in range(n_chunks):
        s = jnp.where(causal, _bdot_nt(kd, q_chunk(c)) * scale, NEG)
        m = jnp.max(s, axis=0, keepdims=True)
        p = jnp.exp(s - m)
        m_ref[c] = m
        l_ref[c] = jnp.sum(p, axis=0, keepdims=True)
        acc_ref[c] = pv(jd, p.astype(BF16))

    def body(j, carry):
        kj = k_tile(j)
        for c in range(n_chunks):
            s = _bdot_nt(kj, q_chunk(c)) * scale
            m_old = m_ref[c]
            m_new = jnp.maximum(m_old, jnp.max(s, axis=0, keepdims=True))
            alpha = jnp.exp(m_old - m_new)
            p = jnp.exp(s - m_new)
            l_ref[c] = alpha * l_ref[c] + jnp.sum(p, axis=0, keepdims=True)
            acc_ref[c] = alpha * acc_ref[c] + pv(j, p.astype(BF16))
            m_ref[c] = m_new
        return carry

    lax.fori_loop(0, jd, body, 0)
    out = jnp.zeros((tq, MLA_HEADS * MLA_V), F32)
    for c in range(n_chunks):
        o_t = acc_ref[c] / l_ref[c]
        for hh in range(MLA_CHUNK_HEADS):
            o_lat = o_t[:, hh * tq:(hh + 1) * tq].T
            out = out + _bdot(o_lat, wuv_ref[c * MLA_CHUNK_HEADS + hh])
    o_ref[...] = out


def _mla_prompt(qcat, kcat, ckvt, wuv, batch, seq, tq, tk):
    nq = seq // tq
    nblk = seq // MOBA_BLOCK
    n_chunks = MLA_HEADS // MLA_CHUNK_HEADS
    cw = MLA_CHUNK_HEADS * tq
    return pl.pallas_call(
        functools.partial(_mla_prompt_kernel, tq=tq, tk=tk, scale=(MLA_NOPE + MLA_ROPE) ** -0.5),
        grid=(batch, nq),
        in_specs=[pl.BlockSpec((MLA_HEADS, tq, KCAT), lambda b, i: (0, b * nq + i, 0)),
                  pl.BlockSpec((seq, KCAT), lambda b, i: (b, 0)),
                  pl.BlockSpec((nblk, MLA_KV_RANK, MOBA_BLOCK), lambda b, i: (b, 0, 0)),
                  _const_spec(wuv.shape)],
        out_specs=pl.BlockSpec((tq, MLA_HEADS * MLA_V), lambda b, i: (b * nq + i, 0)),
        out_shape=jax.ShapeDtypeStruct((batch * seq, MLA_HEADS * MLA_V), F32),
        scratch_shapes=[pltpu.VMEM((n_chunks, 1, cw), F32), pltpu.VMEM((n_chunks, 1, cw), F32),
                        pltpu.VMEM((n_chunks, MLA_KV_RANK, cw), F32)],
        compiler_params=_params(2), name="mla_prompt",
    )(qcat, kcat, ckvt, wuv)


MOBA_BLOCKS_PER_STEP = 4


def _moba_prompt_kernel(q_ref, k_ref, vt_ref, km_ref, slope_ref, o_ref, sel_ref, m_ref, l_ref, acc_ref, *, nb, topk):
    tq = MOBA_BLOCK
    cw = MOBA_REP * tq
    ob = pl.program_id(1)
    km = km_ref[...].astype(BF16)
    rowf = lax.broadcasted_iota(I32, (nb, cw), 0).astype(F32)
    is_past = rowf < ob.astype(F32)

    def q_group(g):
        return q_ref[g * MOBA_REP:(g + 1) * MOBA_REP].reshape(cw, LANES)

    def k_rows(g, jb, n_blocks):
        start = pl.multiple_of(jb * MOBA_BLOCK, MOBA_BLOCK)
        return k_ref[pl.ds(start, n_blocks * MOBA_BLOCK), g * LANES:(g + 1) * LANES]

    def v_block(g, jb):
        return vt_ref[jb, g * HALF:(g + 1) * HALF, :]

    causal = (lax.broadcasted_iota(I32, (MOBA_BLOCK, cw), 0)
              <= (lax.broadcasted_iota(I32, (MOBA_BLOCK, cw), 1) & (tq - 1)))
    for g in range(MOBA_KV_HEADS):
        q = q_group(g)
        gate = jnp.where(is_past, _bdot_nt(km[:, g * LANES:(g + 1) * LANES], q), NEG)
        sel, _ = _top_k_mask(gate, rowf, topk, axis=0)
        sel_ref[g] = jnp.where(is_past, sel, 0.0)
        s = jnp.where(causal, _bdot_nt(k_rows(g, ob, 1), q), NEG)
        m = jnp.max(s, axis=0, keepdims=True)
        p = jnp.exp(s - m)
        m_ref[g] = m
        l_ref[g] = jnp.sum(p, axis=0, keepdims=True)
        acc_ref[g] = jnp.dot(v_block(g, ob), p.astype(BF16), preferred_element_type=F32)

    def step(jb, n_blocks):
        for g in range(MOBA_KV_HEADS):
            s = _bdot_nt(k_rows(g, jb, n_blocks), q_group(g))
            m_old = m_ref[g]
            m_new = m_old
            parts = []
            for t in range(n_blocks):
                far = slope_ref[g] * ((ob - jb - t) * MOBA_BLOCK).astype(F32)
                chosen = sel_ref[g, pl.ds(jb + t, 1), :] > 0.5
                st = s[t * MOBA_BLOCK:(t + 1) * MOBA_BLOCK]
                m_new = jnp.maximum(m_new, jnp.where(chosen, jnp.max(st, axis=0, keepdims=True) - far, NEG))
                parts.append((st, chosen, far))
            alpha = jnp.exp(m_old - m_new)
            l_new = alpha * l_ref[g]
            acc = alpha * acc_ref[g]
            for t, (st, chosen, far) in enumerate(parts):
                p = jnp.exp(st - jnp.where(chosen, m_new + far, MASK_SHIFT))
                l_new = l_new + jnp.sum(p, axis=0, keepdims=True)
                acc = acc + jnp.dot(v_block(g, jb + t), p.astype(BF16), preferred_element_type=F32)
            l_ref[g] = l_new
            acc_ref[g] = acc
            m_ref[g] = m_new

    def wide_body(jj, carry):
        step(jj * MOBA_BLOCKS_PER_STEP, MOBA_BLOCKS_PER_STEP)
        return carry

    def single_body(jb, carry):
        step(jb, 1)
        return carry

    n_wide = ob // MOBA_BLOCKS_PER_STEP
    lax.fori_loop(0, n_wide, wide_body, 0)
    lax.fori_loop(n_wide * MOBA_BLOCKS_PER_STEP, ob, single_body, 0)
    for g in range(MOBA_KV_HEADS):
        o_t = acc_ref[g] / l_ref[g]
        for pr in range(MOBA_REP // 2):
            pair = jnp.concatenate([o_t[:, 2 * pr * tq:(2 * pr + 1) * tq], o_t[:, (2 * pr + 1) * tq:(2 * pr + 2) * tq]], axis=0)
            col = (g * MOBA_REP // 2 + pr) * LANES
            o_ref[:, col:col + LANES] = pair.T


def _moba_prompt(qaug, kaug, vt, km, batch, seq):
    tq = MOBA_BLOCK
    nb = seq // MOBA_BLOCK
    gw = MOBA_KV_HEADS * LANES
    cw = MOBA_REP * tq
    slope_rows = jnp.asarray(np.repeat(np.asarray(SLOPES, np.float32), tq).reshape(MOBA_KV_HEADS, 1, cw))
    return pl.pallas_call(
        functools.partial(_moba_prompt_kernel, nb=nb, topk=min(MOBA_TOPK, nb)),
        grid=(batch, nb),
        in_specs=[pl.BlockSpec((MOBA_HEADS, tq, LANES), lambda b, i: (0, b * nb + i, 0)),
                  pl.BlockSpec((seq, gw), lambda b, i: (b, 0)),
                  pl.BlockSpec((nb, LANES, MOBA_BLOCK), lambda b, i: (b, 0, 0)),
                  pl.BlockSpec((nb, gw), lambda b, i: (b, 0)),
                  _const_spec((MOBA_KV_HEADS, 1, cw))],
        out_specs=pl.BlockSpec((tq, MOBA_HEADS * MOBA_HD), lambda b, i: (b * nb + i, 0)),
        out_shape=jax.ShapeDtypeStruct((batch * seq, MOBA_HEADS * MOBA_HD), F32),
        scratch_shapes=[pltpu.VMEM((MOBA_KV_HEADS, nb, cw), F32), pltpu.VMEM((MOBA_KV_HEADS, 1, cw), F32),
                        pltpu.VMEM((MOBA_KV_HEADS, 1, cw), F32), pltpu.VMEM((MOBA_KV_HEADS, MOBA_HD, cw), F32)],
        compiler_params=_params(2), name="moba_prompt",
    )(qaug, kaug, vt, km, slope_rows)


HALO = 16


def _pool_prompt_kernel(u_ref, halo_ref, w_ref, sc_ref, o_ref, ext_ref, *, tiles_per_seq):
    tm = u_ref.shape[0]
    i = pl.program_id(0) % tiles_per_seq
    u = u_ref[...]
    ext_ref[0:HALO, :] = jnp.where(i > 0, halo_ref[...], 0.0)
    ext_ref[HALO:HALO + tm, :] = u
    pos = i * tm + lax.broadcasted_iota(I32, (tm, 1), 0)
    parts = []
    for g, w in enumerate(POOL_WINDOWS):
        cols = slice(g * POOL_GROUP, (g + 1) * POOL_GROUP)
        tot = u[:, cols]
        for k in range(1, w):
            tot = tot + ext_ref[HALO - k:HALO - k + tm, cols]
        cnt = jnp.minimum(pos + 1, w).astype(F32)
        parts.append(_bdot(tot / cnt - u[:, cols], w_ref[g]))
    o_ref[...] = jnp.concatenate(parts, axis=-1) * sc_ref[...]


def _pool_prompt(u, w_pool, scale, seq, tm):
    n = u.shape[0]
    tps = seq // tm
    hpt = tm // HALO
    return pl.pallas_call(
        functools.partial(_pool_prompt_kernel, tiles_per_seq=tps),
        grid=(n // tm,),
        in_specs=[pl.BlockSpec((tm, POOL_WIDTH), lambda i: (i, 0)),
                  pl.BlockSpec((HALO, POOL_WIDTH), lambda i: (jnp.maximum(i * hpt - 1, 0), 0)),
                  _const_spec(w_pool.shape), _const_spec((1, POOL_WIDTH))],
        out_specs=pl.BlockSpec((tm, POOL_WIDTH), lambda i: (i, 0)),
        out_shape=jax.ShapeDtypeStruct((n, POOL_WIDTH), F32),
        scratch_shapes=[pltpu.VMEM((HALO + tm, POOL_WIDTH), F32)],
        compiler_params=_params(), name="pool_prompt",
    )(u, u, w_pool, scale)


def _pool_sample_kernel(st_ref, u_ref, w_ref, sc_ref, o_ref, *, start_pos):
    u = u_ref[...]
    parts = []
    for g, w in enumerate(POOL_WINDOWS):
        cols = slice(g * POOL_GROUP, (g + 1) * POOL_GROUP)
        tot = u[:, cols]
        for k in range(1, w):
            tot = tot + st_ref[POOL_BUF - k][:, cols]
        parts.append(_bdot(tot / float(min(start_pos + 1, w)) - u[:, cols], w_ref[g]))
    o_ref[...] = jnp.concatenate(parts, axis=-1) * sc_ref[...]


def _pool_sample(state_t, u, w_pool, scale, start_pos):
    n = u.shape[0]
    return pl.pallas_call(
        functools.partial(_pool_sample_kernel, start_pos=start_pos),
        grid=(1,),
        in_specs=[_const_spec(state_t.shape), _const_spec(u.shape), _const_spec(w_pool.shape), _const_spec((1, POOL_WIDTH))],
        out_specs=_const_spec((n, POOL_WIDTH)),
        out_shape=jax.ShapeDtypeStruct((n, POOL_WIDTH), F32),
        compiler_params=_params(), name="pool_sample",
    )(state_t, u, w_pool, scale)


def _mem_kv_kernel(mem_ref, g_ref, w_ref, k_ref, v_ref):
    kv = _bdot(_rms(mem_ref[...], g_ref[...]), w_ref[...])
    half = MEM_HEADS * MEM_HD
    k_ref[...] = kv[:, :half]
    v_ref[...] = kv[:, half:]


def _mem_kv(mem, g, w):
    n = mem.shape[0]
    half = MEM_HEADS * MEM_HD
    return pl.pallas_call(
        _mem_kv_kernel, grid=(1,),
        in_specs=[_const_spec(mem.shape), _const_spec((1, D_MODEL)), _const_spec(w.shape)],
        out_specs=[_const_spec((n, half)), _const_spec((n, half))],
        out_shape=[jax.ShapeDtypeStruct((n, half), F32), jax.ShapeDtypeStruct((n, half), F32)],
        compiler_params=_params(), name="mem_kv",
    )(mem, g, w)


def _mem_prompt_kernel(q_ref, k_ref, v_ref, o_ref):
    q = q_ref[...]
    outs = []
    for hd in range(MEM_HEADS):
        cols = slice(hd * MEM_HD, (hd + 1) * MEM_HD)
        s = _bdot_nt(q[:, cols], k_ref[:, cols]) * (MEM_HD ** -0.5)
        e = jnp.exp(s - jnp.max(s, axis=-1, keepdims=True))
        p = e / jnp.sum(e, axis=-1, keepdims=True)
        outs.append(_bdot(p, v_ref[:, cols]))
    o_ref[...] = jnp.concatenate(outs, axis=-1)


def _mem_prompt(qmem, mem_k, mem_v, seq, tm):
    n = qmem.shape[0]
    tps = seq // tm
    m_tok = mem_k.shape[0] // (n // seq)
    width = MEM_HEADS * MEM_HD
    return pl.pallas_call(
        _mem_prompt_kernel, grid=(n // tm,),
        in_specs=[pl.BlockSpec((tm, width), lambda i: (i, 0)),
                  pl.BlockSpec((m_tok, width), lambda i: (i // tps, 0)),
                  pl.BlockSpec((m_tok, width), lambda i: (i // tps, 0))],
        out_specs=pl.BlockSpec((tm, width), lambda i: (i, 0)),
        out_shape=jax.ShapeDtypeStruct((n, width), F32),
        compiler_params=_params(), name="mem_prompt",
    )(qmem, mem_k, mem_v)


def _mem_sample_kernel(q_ref, k_ref, v_ref, o_ref):
    sb = q_ref.shape[0]
    q = q_ref[...].astype(F32)
    for b in range(sb):
        s = jnp.sum(k_ref[b] * q[b][None], axis=-1, keepdims=True) * (MEM_HD ** -0.5)
        e = jnp.exp(s - jnp.max(s, axis=0, keepdims=True))
        o_ref[b] = jnp.sum(e * v_ref[b], axis=0) / jnp.sum(e, axis=0)


def _mem_sample(qmem, cache_k, cache_v, layer, sb):
    _, n, m_tok, _, _ = cache_k.shape
    blk = pl.BlockSpec((None, sb, m_tok, MEM_HEADS, MEM_HD), lambda i: (layer, i, 0, 0, 0))
    row = pl.BlockSpec((sb, MEM_HEADS, MEM_HD), lambda i: (i, 0, 0))
    return pl.pallas_call(
        _mem_sample_kernel, grid=(n // sb,),
        in_specs=[row, blk, blk], out_specs=row,
        out_shape=jax.ShapeDtypeStruct((n, MEM_HEADS, MEM_HD), F32),
        compiler_params=_params(), name="mem_sample",
    )(qmem, cache_k, cache_v)


def _merge_kernel(x_ref, g_ref, wz_ref, wgl_ref, wbr_ref, wout_ref, fn_ref, b0_ref, b1_ref, b2_ref, b3_ref, o_ref, *, final):
    x = x_ref[...]
    h = _rms(x, g_ref[...]).astype(BF16)
    acc = jnp.zeros(x.shape, F32)
    for b, br in enumerate((b0_ref, b1_ref, b2_ref, b3_ref)):
        z = jnp.dot(h, wz_ref[:, b * BRANCH_WIDTH:(b + 1) * BRANCH_WIDTH], preferred_element_type=F32)
        gl = jnp.dot(h, wgl_ref[:, b * D_MODEL:(b + 1) * D_MODEL], preferred_element_type=F32)
        o = br[...] * (z * _sigmoid(z))
        acc = acc + _sigmoid(gl) * _bdot(o, wbr_ref[b])
    y = x + _bdot(acc, wout_ref[...])
    if final:
        y = _rms(y, fn_ref[...])
    o_ref[...] = y


def _merge(x, lw, final_norm, branches, tm, final):
    n = x.shape[0]
    row = lambda w: pl.BlockSpec((tm, w), lambda i: (i, 0))
    return pl.pallas_call(
        functools.partial(_merge_kernel, final=final),
        grid=(n // tm,),
        in_specs=[row(D_MODEL), _const_spec((1, D_MODEL)), _const_spec(lw["wz"].shape), _const_spec(lw["wgl"].shape),
                  _const_spec(lw["wbr"].shape), _const_spec(lw["wout"].shape), _const_spec((1, D_MODEL))]
                 + [row(BRANCH_WIDTH)] * N_BRANCH,
        out_specs=row(D_MODEL),
        out_shape=jax.ShapeDtypeStruct((n, D_MODEL), F32),
        compiler_params=_params(), name="merge",
    )(x, lw["ln_g"], lw["wz"], lw["wgl"], lw["wbr"], lw["wout"], final_norm, *branches)


N_SLOTS = 3
PREFETCH = N_SLOTS - 1
MAX_CHUNK_PAGES = 32


def _chunking(n_pages):
    ch = max(c for c in range(1, MAX_CHUNK_PAGES + 1) if n_pages % c == 0 and n_pages // c >= PREFETCH)
    return ch, n_pages // ch


def _mla_sample_kernel(pt_ref, q_ref, kn_ref, ckv_hbm, krt_hbm, o_ref, kbuf, rbuf, sem, *, layer, ch, nch, scale):
    b = pl.program_id(0)
    total = pl.num_programs(0) * nch

    def copies(t):
        bb = t // nch
        c = t % nch
        slot = t % N_SLOTS
        out = []
        for p in range(ch):
            page = pt_ref[bb, c * ch + p]
            out.append(pltpu.make_async_copy(ckv_hbm.at[layer, page], kbuf.at[slot, pl.ds(p * PAGE_SIZE, PAGE_SIZE)],
                                             sem.at[0, slot]))
            out.append(pltpu.make_async_copy(krt_hbm.at[layer, page], rbuf.at[slot, p], sem.at[1, slot]))
        return out

    @pl.when(b == 0)
    def _():
        for t in range(PREFETCH):
            for cp in copies(t):
                cp.start()

    q = q_ref[0]
    q_lat = q[:, :MLA_KV_RANK]
    q_rope = q[:, MLA_KV_RANK:MLA_KV_RANK + MLA_ROPE]
    kn = kn_ref[0]
    m = jnp.sum(q.astype(F32) * kn.astype(F32), axis=-1, keepdims=True) * scale
    l = jnp.ones_like(m)
    acc = jnp.broadcast_to(kn[:, :MLA_KV_RANK].astype(F32), (MLA_HEADS, MLA_KV_RANK))
    for c in range(nch):
        t = b * nch + c
        slot = t % N_SLOTS

        @pl.when(t + PREFETCH < total)
        def _():
            for cp in copies(t + PREFETCH):
                cp.start()

        for cp in copies(t):
            cp.wait()
        kb = kbuf[slot].astype(BF16)
        s_rope = jnp.concatenate([_bdot(q_rope, rbuf[slot, p]) for p in range(ch)], axis=-1)
        s = (_bdot_nt(q_lat, kb) + s_rope) * scale
        m_new = jnp.maximum(m, jnp.max(s, axis=-1, keepdims=True))
        alpha = jnp.exp(m - m_new)
        p = jnp.exp(s - m_new)
        l = alpha * l + jnp.sum(p, axis=-1, keepdims=True)
        acc = alpha * acc + jnp.dot(p.astype(BF16), kb, preferred_element_type=F32)
        m = m_new
    o_ref[0] = acc / l


def _mla_sample(page_table, q, knew, cache_ckv, cache_krt, layer):
    ndb, n_pages = page_table.shape
    ch, nch = _chunking(n_pages)
    rows = ch * PAGE_SIZE
    return pl.pallas_call(
        functools.partial(_mla_sample_kernel, layer=layer, ch=ch, nch=nch, scale=(MLA_NOPE + MLA_ROPE) ** -0.5),
        grid_spec=pltpu.PrefetchScalarGridSpec(
            num_scalar_prefetch=1, grid=(ndb,),
            in_specs=[pl.BlockSpec((1, MLA_HEADS, KCAT), lambda b, pt: (b, 0, 0)),
                      pl.BlockSpec((1, 1, KCAT), lambda b, pt: (b, 0, 0)),
                      pl.BlockSpec(memory_space=pl.ANY), pl.BlockSpec(memory_space=pl.ANY)],
            out_specs=pl.BlockSpec((1, MLA_HEADS, MLA_KV_RANK), lambda b, pt: (b, 0, 0)),
            scratch_shapes=[pltpu.VMEM((N_SLOTS, rows, MLA_KV_RANK), F32), pltpu.VMEM((N_SLOTS, ch, MLA_ROPE, PAGE_SIZE), F32),
                            pltpu.SemaphoreType.DMA((2, N_SLOTS))]),
        out_shape=jax.ShapeDtypeStruct((ndb, MLA_HEADS, MLA_KV_RANK), F32),
        compiler_params=_params(), name="mla_sample",
    )(page_table, q, knew, cache_ckv, cache_krt)


def _uv_kernel(o_ref, wuv_ref, y_ref):
    out = jnp.zeros(y_ref.shape, F32)
    for hd in range(MLA_HEADS):
        out = out + _bdot(o_ref[hd], wuv_ref[hd])
    y_ref[...] = out


def _uv_project(o_lat, wuv):
    n = o_lat.shape[1]
    return pl.pallas_call(
        _uv_kernel, grid=(1,),
        in_specs=[_const_spec(o_lat.shape), _const_spec(wuv.shape)],
        out_specs=_const_spec((n, MLA_HEADS * MLA_V)),
        out_shape=jax.ShapeDtypeStruct((n, MLA_HEADS * MLA_V), F32),
        compiler_params=_params(), name="mla_uv",
    )(o_lat, wuv)


def _moba_gate_kernel(pt_ref, q_ref, k_hbm, idx_ref, kbuf, km_ref, sem, *, layer, ch, nch, own, topk):
    b = pl.program_id(0)
    total = pl.num_programs(0) * nch

    def copies(t):
        bb = t // nch
        c = t % nch
        slot = t % N_SLOTS
        return [pltpu.make_async_copy(k_hbm.at[layer, pt_ref[bb, c * ch + p]], kbuf.at[slot, p], sem.at[slot])
                for p in range(ch)]

    @pl.when(b == 0)
    def _():
        for t in range(PREFETCH):
            for cp in copies(t):
                cp.start()

    bpc = ch // MOBA_PPB
    for c in range(nch):
        t = b * nch + c
        slot = t % N_SLOTS

        @pl.when(t + PREFETCH < total)
        def _():
            for cp in copies(t + PREFETCH):
                cp.start()

        for cp in copies(t):
            cp.wait()
        for bi in range(bpc):
            n = c * bpc + bi
            for g in range(MOBA_KV_HEADS):
                tot = kbuf[slot, bi * MOBA_PPB, g]
                for j in range(1, MOBA_PPB):
                    tot = tot + kbuf[slot, bi * MOBA_PPB + j, g]
                km_ref[g, :, n:n + 1] = jnp.sum(tot, axis=-1, keepdims=True) * (1.0 / MOBA_BLOCK)
    q = q_ref[0]
    nblk = km_ref.shape[2]
    g0 = _bdot(q, km_ref[0])
    g1 = _bdot(q, km_ref[1])
    head = lax.broadcasted_iota(I32, (MOBA_HEADS, nblk), 0)
    colf = lax.broadcasted_iota(I32, (MOBA_HEADS, nblk), 1).astype(F32)
    gate = jnp.where(head < MOBA_REP, g0, g1)
    gate = jnp.where(colf < float(own), gate, NEG)
    _, picked = _top_k_mask(gate, colf, topk)
    lane = lax.broadcasted_iota(I32, (MOBA_HEADS, LANES), 1)
    out = jnp.zeros((MOBA_HEADS, LANES), F32)
    for t, idx in enumerate(picked):
        out = jnp.where(lane == t, idx, out)
    idx_ref[0] = out.astype(I32)


def _moba_gate(page_table, q, cache_kt, layer, n_full, own, topk):
    ndb = page_table.shape[0]
    ch, nch = _chunking(n_full * MOBA_PPB)
    return pl.pallas_call(
        functools.partial(_moba_gate_kernel, layer=layer, ch=ch, nch=nch, own=own, topk=topk),
        grid_spec=pltpu.PrefetchScalarGridSpec(
            num_scalar_prefetch=1, grid=(ndb,),
            in_specs=[pl.BlockSpec((1, MOBA_HEADS, MOBA_HD), lambda b, pt: (b, 0, 0)),
                      pl.BlockSpec(memory_space=pl.ANY)],
            out_specs=pl.BlockSpec((1, MOBA_HEADS, LANES), lambda b, pt: (b, 0, 0)),
            scratch_shapes=[pltpu.VMEM((N_SLOTS, ch, MOBA_KV_HEADS, MOBA_HD, PAGE_SIZE), F32),
                            pltpu.VMEM((MOBA_KV_HEADS, MOBA_HD, n_full), F32),
                            pltpu.SemaphoreType.DMA((N_SLOTS,))]),
        out_shape=jax.ShapeDtypeStruct((ndb, MOBA_HEADS, LANES), I32),
        compiler_params=_params(), name="moba_gate",
    )(page_table, q, cache_kt)


def _moba_sample_kernel(pt_ref, idx_ref, qt_ref, knt_ref, vnt_ref, k_hbm, v_hbm, ot_ref, kbuf, vbuf, sem,
                        *, layer, topk, own, q_pos):
    b = pl.program_id(0)
    nb = pl.num_programs(0)

    def copies(bb, slot):
        out = []
        for hd in range(MOBA_HEADS):
            g = hd // MOBA_REP
            for t in range(topk):
                blk = idx_ref[bb, hd * topk + t]
                for j in range(MOBA_PPB):
                    page = pt_ref[bb, blk * MOBA_PPB + j]
                    dst = (hd * topk + t) * MOBA_PPB + j
                    out.append(pltpu.make_async_copy(k_hbm.at[layer, page, g], kbuf.at[slot, dst], sem.at[0, slot]))
                    out.append(pltpu.make_async_copy(v_hbm.at[layer, page, g], vbuf.at[slot, dst], sem.at[1, slot]))
        return out

    slot = b % 2

    @pl.when(b == 0)
    def _():
        for cp in copies(0, 0):
            cp.start()

    @pl.when(b + 1 < nb)
    def _():
        for cp in copies(b + 1, 1 - slot):
            cp.start()

    for cp in copies(b, slot):
        cp.wait()

    qt = qt_ref[0]
    knt = knt_ref[0]
    vnt = vnt_ref[0]
    lane = lax.broadcasted_iota(I32, (1, PAGE_SIZE), 1)
    for hd in range(MOBA_HEADS):
        g = hd // MOBA_REP
        qcol = qt[:, hd:hd + 1]
        pieces = []
        for t in range(topk):
            blk = idx_ref[b, hd * topk + t]
            for j in range(MOBA_PPB):
                kt = kbuf[slot, (hd * topk + t) * MOBA_PPB + j]
                dist = (q_pos - (blk * MOBA_BLOCK + j * PAGE_SIZE + lane)).astype(F32)
                s = jnp.sum(kt * qcol, axis=0, keepdims=True) - SLOPES[hd] * dist
                pieces.append(jnp.where(blk < own, s, NEG))
        s_self = jnp.sum(qcol * knt[:, g:g + 1], axis=0, keepdims=True)
        smax = pieces[0]
        for s in pieces[1:]:
            smax = jnp.maximum(smax, s)
        m = jnp.maximum(jnp.max(smax, axis=-1, keepdims=True), s_self)
        p_self = jnp.exp(s_self - m)
        den = p_self
        wsum = jnp.zeros((MOBA_HD, PAGE_SIZE), F32)
        for pi, s in enumerate(pieces):
            p = jnp.exp(s - m)
            den = den + jnp.sum(p, axis=-1, keepdims=True)
            wsum = wsum + vbuf[slot, hd * topk * MOBA_PPB + pi] * p
        o = (jnp.sum(wsum, axis=-1, keepdims=True) + p_self * vnt[:, g:g + 1]) / den
        ot_ref[0, :, hd:hd + 1] = o


def _moba_sample(page_table, idx, qt, knt, vnt, cache_kt, cache_vt, layer, topk, own, q_pos):
    ndb = page_table.shape[0]
    n_buf = MOBA_HEADS * topk * MOBA_PPB
    col = lambda w: pl.BlockSpec((1, MOBA_HD, w), lambda b, pt, ix: (b, 0, 0))
    return pl.pallas_call(
        functools.partial(_moba_sample_kernel, layer=layer, topk=topk, own=own, q_pos=q_pos),
        grid_spec=pltpu.PrefetchScalarGridSpec(
            num_scalar_prefetch=2, grid=(ndb,),
            in_specs=[col(MOBA_HEADS), col(MOBA_KV_HEADS), col(MOBA_KV_HEADS),
                      pl.BlockSpec(memory_space=pl.ANY), pl.BlockSpec(memory_space=pl.ANY)],
            out_specs=col(MOBA_HEADS),
            scratch_shapes=[pltpu.VMEM((2, n_buf, MOBA_HD, PAGE_SIZE), F32),
                            pltpu.VMEM((2, n_buf, MOBA_HD, PAGE_SIZE), F32),
                            pltpu.SemaphoreType.DMA((2, 2))]),
        out_shape=jax.ShapeDtypeStruct((ndb, MOBA_HD, MOBA_HEADS), F32),
        compiler_params=_params(), name="moba_sample",
    )(page_table, idx, qt, knt, vnt, cache_kt, cache_vt)


def _rope_tables(pos):
    half = MLA_ROPE // 2
    inv = ROPE_BASE ** (-jnp.arange(half, dtype=F32) / half)
    ang = pos.astype(F32)[:, None] * inv[None, :]
    cos, sin = jnp.cos(ang), jnp.sin(ang)
    reps = 2 * LANES // MLA_ROPE
    return jnp.tile(jnp.concatenate([cos, cos], axis=-1), (1, reps)), jnp.tile(jnp.concatenate([-sin, sin], axis=-1), (1, reps))


def _pad_lanes(w):
    return jnp.pad(w, ((0, 0), (0, LANES - w.shape[1])))


def _layer_weights(l, ln_g, w_in, mla_q_norm, mla_w_uq, mla_kv_norm, mla_w_uk, mla_w_uv, pool_w, pool_scale,
                   mem_norm, w_mem_kv, w_branch, w_out):
    half = MLA_ROPE // 2
    swap = jnp.concatenate([jnp.arange(half, MLA_ROPE), jnp.arange(half)])
    offs = [0]
    for s in IN_SIZES:
        offs.append(offs[-1] + s)
    c_q, c_kv, k_rope, u, q_mb, k_mb, v_mb, q_mem, z, gl = [w_in[l][:, offs[i]:offs[i + 1]] for i in range(len(IN_SIZES))]
    wa = jnp.concatenate([c_q, c_kv, u, q_mb, k_mb, v_mb, q_mem, _pad_lanes(k_rope), _pad_lanes(k_rope[:, swap])], axis=1)
    uq = mla_w_uq[l]
    qr = uq.shape[0]
    wuq = jnp.concatenate([uq[:, :, :MLA_NOPE].reshape(qr, -1), uq[:, :, MLA_NOPE:].reshape(qr, -1),
                           uq[:, :, MLA_NOPE:][:, :, swap].reshape(qr, -1)], axis=1)
    mh = jnp.zeros((MLA_HEADS, 2 * LANES, KCAT), F32)
    eye = jnp.eye(MLA_ROPE, dtype=F32)
    for hd in range(MLA_HEADS):
        r0 = (hd * MLA_NOPE) % LANES
        mh = mh.at[hd, r0:r0 + MLA_NOPE, :MLA_KV_RANK].set(mla_w_uk[l][:, hd, :].T)
        r1 = LANES + (hd * MLA_ROPE) % LANES
        mh = mh.at[hd, r1:r1 + MLA_ROPE, MLA_KV_RANK:MLA_KV_RANK + MLA_ROPE].set(eye)
    wuv = jnp.zeros((MLA_HEADS, MLA_KV_RANK, MLA_HEADS * MLA_V), F32)
    for hd in range(MLA_HEADS):
        wuv = wuv.at[hd, :, hd * MLA_V:(hd + 1) * MLA_V].set(mla_w_uv[l][:, hd, :])
    return dict(
        ln_g=ln_g[l][None, :], wa=wa.astype(BF16), q_norm=mla_q_norm[l][None, :], kv_norm=mla_kv_norm[l][None, :],
        wuq=wuq.astype(BF16), mh=mh.astype(BF16), wuv=wuv.astype(BF16), wz=z.astype(BF16), wgl=gl.astype(BF16),
        wbr=w_branch[l].astype(BF16), wout=w_out[l].astype(BF16), pool_w=pool_w[l].astype(BF16),
        pool_scale=pool_scale[l][None, :], mem_norm=mem_norm[l][None, :], w_mem_kv=w_mem_kv[l].astype(BF16))


def kernel(x_prompt, x_sample, cache_mla_ckv, cache_mla_krope, cache_moba_k, cache_moba_v, state_pool, cache_mem_k, cache_mem_v, page_table, mem_prompt, ln_g, w_in, mla_q_norm, mla_w_uq, mla_kv_norm, mla_w_uk, mla_w_uv, pool_w, pool_scale, mem_norm, w_mem_kv, w_branch, w_out, final_norm):
    batch, seq, _ = x_prompt.shape
    ndb, t_new, _ = x_sample.shape
    depth = ln_g.shape[0]
    n_pages = page_table.shape[1]
    past = n_pages * PAGE_SIZE
    m_tok = mem_prompt.shape[1]
    assert t_new == 1, "one new token per sample"
    assert past % MOBA_BLOCK == 0, "no partially filled cached MoBA block"
    assert seq % MOBA_BLOCK == 0 and ndb % 8 == 0 and MOBA_PPB * PAGE_SIZE == MOBA_BLOCK
    tm = MOBA_BLOCK
    tq = min(512, seq)
    tk = min(512, seq)
    n_full = past // MOBA_BLOCK
    own_s = past // MOBA_BLOCK
    topk_s = min(MOBA_TOPK, n_full)
    cos_p, sin_p = _rope_tables(jnp.arange(seq))
    cos_s, sin_s = _rope_tables(jnp.full((ndb,), past))
    fnorm = final_norm[None, :]
    cache_krt = cache_mla_krope.transpose(0, 1, 3, 2)
    cache_kt = cache_moba_k.transpose(0, 1, 3, 4, 2)
    cache_vt = cache_moba_v.transpose(0, 1, 3, 4, 2)

    xp = x_prompt.reshape(batch * seq, D_MODEL)
    xs = x_sample.reshape(ndb, D_MODEL)
    mem_flat = mem_prompt.reshape(batch * m_tok, D_MODEL)
    outs = {k: [] for k in ("ckv_p", "kr_p", "mbk_p", "mbv_p", "pool_p", "memk_p", "memv_p",
                            "ckv_s", "kr_s", "mbk_s", "mbv_s", "pool_s")}
    for l in range(depth):
        lw = _layer_weights(l, ln_g, w_in, mla_q_norm, mla_w_uq, mla_kv_norm, mla_w_uk, mla_w_uv, pool_w, pool_scale,
                            mem_norm, w_mem_kv, w_branch, w_out)
        last = l == depth - 1
        pj = _project(xp, lw, cos_p, sin_p, tm, True)
        o_mla = _mla_prompt(pj["qcat"], pj["kcat"], pj["ckvt"], lw["wuv"], batch, seq, tq, tk)
        o_pool = _pool_prompt(pj["u"], lw["pool_w"], lw["pool_scale"], seq, tm)
        km = pj["km"].reshape(batch * seq // MOBA_BLOCK, MOBA_KV_HEADS * LANES)
        o_moba = _moba_prompt(pj["qaug"], pj["kaug"], pj["vt"], km, batch, seq)
        mem_k, mem_v = _mem_kv(mem_flat, lw["mem_norm"], lw["w_mem_kv"])
        o_mem = _mem_prompt(pj["qmem"], mem_k, mem_v, seq, tm)
        xp = _merge(xp, lw, fnorm, (o_mla, o_pool, o_moba, o_mem), tm, last)
        outs["ckv_p"].append(pj["ckv"].reshape(batch, seq, MLA_KV_RANK))
        outs["kr_p"].append(pj["kr"].reshape(batch, seq, MLA_ROPE))
        outs["mbk_p"].append(pj["kmb"].reshape(batch, seq, MOBA_KV_HEADS, MOBA_HD))
        outs["mbv_p"].append(pj["vmb"].reshape(batch, seq, MOBA_KV_HEADS, MOBA_HD))
        outs["pool_p"].append(pj["u"].reshape(batch, seq, POOL_WIDTH)[:, seq - POOL_BUF:])
        outs["memk_p"].append(mem_k.reshape(batch, m_tok, MEM_HEADS, MEM_HD))
        outs["memv_p"].append(mem_v.reshape(batch, m_tok, MEM_HEADS, MEM_HD))
        sj = _project(xs, lw, cos_s, sin_s, ndb, False)
        o_lat = _mla_sample(page_table, sj["qcat"].transpose(1, 0, 2), sj["kcat"][:, None, :], cache_mla_ckv,
                            cache_krt, l)
        o_mla = _uv_project(o_lat.transpose(1, 0, 2), lw["wuv"])
        u_s = sj["u"]
        o_pool = _pool_sample(state_pool[l].transpose(1, 0, 2), u_s, lw["pool_w"], lw["pool_scale"], past)
        q_heads = sj["qmb"].reshape(ndb, MOBA_HEADS, MOBA_HD)
        knew = sj["kmb"].reshape(ndb, MOBA_KV_HEADS, MOBA_HD)
        vnew = sj["vmb"].reshape(ndb, MOBA_KV_HEADS, MOBA_HD)
        idx = _moba_gate(page_table, q_heads, cache_kt, l, n_full, own_s, topk_s)
        idx = idx[:, :, :topk_s].reshape(ndb, MOBA_HEADS * topk_s)
        o_moba_t = _moba_sample(page_table, idx, q_heads.astype(F32).transpose(0, 2, 1), knew.transpose(0, 2, 1),
                                vnew.transpose(0, 2, 1), cache_kt, cache_vt, l, topk_s, own_s, past)
        o_moba = o_moba_t.transpose(0, 2, 1).reshape(ndb, MOBA_HEADS * MOBA_HD)
        o_mem = _mem_sample(sj["qmem"].reshape(ndb, MEM_HEADS, MEM_HD), cache_mem_k, cache_mem_v, l, 8)
        o_mem = o_mem.reshape(ndb, MEM_HEADS * MEM_HD)
        xs = _merge(xs, lw, fnorm, (o_mla, o_pool, o_moba, o_mem), ndb, last)
        outs["ckv_s"].append(sj["ckv"][:, None, :])
        outs["kr_s"].append(sj["kr"][:, None, :])
        outs["mbk_s"].append(knew[:, None])
        outs["mbv_s"].append(vnew[:, None])
        outs["pool_s"].append(jnp.concatenate([state_pool[l][:, 1:], u_s[:, None, :]], axis=1))
    st = lambda k: jnp.stack(outs[k])
    return (xp.reshape(batch, seq, D_MODEL), xs.reshape(ndb, t_new, D_MODEL), st("ckv_p"), st("kr_p"), st("mbk_p"),
            st("mbv_p"), st("pool_p"), st("memk_p"), st("memv_p"), st("ckv_s"), st("kr_s"), st("mbk_s"), st("mbv_s"),
            st("pool_s"))
```
